```python
import math
import jax
import jax.numpy as jnp
from jax import lax
import numpy as np

D_MODEL = 1024
BATCH = 8
SEQ = 4096
DEPTH = 1

CHUNK = 64
MIX_WIDTH = D_MODEL
DN_WIDTH = MIX_WIDTH // 2
DN_HEAD_DIM = 128
DN_HEADS = DN_WIDTH // DN_HEAD_DIM
CONV_WIDTH = 4
CA_WIDTH = MIX_WIDTH - DN_WIDTH
CA_HEAD_DIM = 64
CA_HEADS = CA_WIDTH // CA_HEAD_DIM
CA_LEFT_CHUNKS = 8
MAX_REL_DIST = 256
N_EXPERTS = 32
TOP_K = 4
D_EXPERT = D_MODEL
SWIGLU_LIMIT = 7.0
SWIGLU_ALPHA = 1.702
EXPERT_BLOCK = 128
NORM_EPS = 1e-6
IN_SPLITS = (DN_WIDTH, DN_WIDTH, DN_WIDTH, DN_WIDTH, DN_HEADS, DN_HEADS, CA_WIDTH, CA_WIDTH, CA_WIDTH)
IN_WIDTH = sum(IN_SPLITS)

kernel_name = 'hybrid_deltanet_chunkattn_moe_block'


def rmsnorm(x, w):
    xf = x.astype(jnp.float32)
    y = xf * lax.rsqrt(jnp.mean(xf * xf, axis=-1, keepdims=True) + NORM_EPS)
    return (y * w.astype(jnp.float32)).astype(x.dtype)


def l2norm(x):
    return x * lax.rsqrt(jnp.sum(x * x, axis=-1, keepdims=True) + NORM_EPS)


def causal_depthwise_conv(x, w):
    k, c = w.shape
    return lax.conv_general_dilated(
        x, w[:, None, :].astype(x.dtype), window_strides=(1,), padding=[(k - 1, 0)],
        dimension_numbers=('NWC', 'WIO', 'NWC'), feature_group_count=c)


def gated_deltanet(q, k, v, z, b, a, a_log, dt_bias, norm_w):
    f32 = jnp.float32
    bsz, seq, _ = q.shape
    n_chunks = seq // CHUNK
    H, dh = DN_HEADS, DN_HEAD_DIM

    def heads(t):
        return t.astype(f32).reshape(bsz, seq, H, dh)

    def chunks(t):
        return t.reshape(bsz, n_chunks, CHUNK, H, -1).transpose(0, 3, 1, 2, 4)

    qh = chunks(l2norm(heads(q)) * (dh ** -0.5))
    kh = chunks(l2norm(heads(k)))
    vh = chunks(heads(v))
    beta = jax.nn.sigmoid(b.astype(f32))
    g = -jnp.exp(a_log.astype(f32)) * jax.nn.softplus(a.astype(f32) + dt_bias.astype(f32))
    beta = chunks(beta[..., None])[..., 0]
    gc = jnp.cumsum(chunks(g[..., None])[..., 0], axis=-1)
    causal = jnp.tril(jnp.ones((CHUNK, CHUNK), dtype=bool))
    strict = jnp.tril(jnp.ones((CHUNK, CHUNK), dtype=bool), k=-1)
    decay = jnp.exp(jnp.where(causal, gc[..., :, None] - gc[..., None, :], -jnp.inf))
    k_beta = kh * beta[..., None]
    lower = jnp.where(strict, jnp.einsum('bhnid,bhnjd->bhnij', k_beta, kh) * decay, 0.0)
    eye = jnp.eye(CHUNK, dtype=f32)
    rhs = jnp.concatenate([vh * beta[..., None], k_beta * jnp.exp(gc)[..., None]], axis=-1)
    sol = lax.linalg.triangular_solve(lower + eye, rhs, left_side=True, lower=True, unit_diagonal=True)
    u, w = sol[..., :dh], sol[..., dh:]
    qk = jnp.einsum('bhnid,bhnjd->bhnij', qh, kh) * decay
    q_dec = qh * jnp.exp(gc)[..., None]
    k_dec = kh * jnp.exp(gc[..., -1:] - gc)[..., None]
    chunk_decay = jnp.exp(gc[..., -1])

    def step(state, inp):
        u_n, w_n, qd_n, kd_n, qk_n, cd_n = inp
        v_new = u_n - jnp.einsum('bhck,bhkv->bhcv', w_n, state)
        o_n = jnp.einsum('bhck,bhkv->bhcv', qd_n, state) + jnp.einsum('bhij,bhjv->bhiv', qk_n, v_new)
        state = state * cd_n[..., None, None] + jnp.einsum('bhck,bhcv->bhkv', kd_n, v_new)
        return state, o_n

    xs = (jnp.moveaxis(u, 2, 0), jnp.moveaxis(w, 2, 0), jnp.moveaxis(q_dec, 2, 0),
          jnp.moveaxis(k_dec, 2, 0), jnp.moveaxis(qk, 2, 0), jnp.moveaxis(chunk_decay, 2, 0))
    state0 = jnp.zeros((bsz, H, dh, dh), f32)
    _, o = lax.scan(step, state0, xs)
    o = o.transpose(1, 0, 3, 2, 4).reshape(bsz, seq, H, dh)
    o = o * lax.rsqrt(jnp.mean(o * o, axis=-1, keepdims=True) + NORM_EPS) * norm_w.astype(f32)
    o = o * jax.nn.silu(heads(z))
    return o.reshape(bsz, seq, DN_WIDTH).astype(q.dtype)


def chunk_band_attention(q, k, v, rel_bias, norm_w):
    f32 = jnp.float32
    bsz, seq, _ = q.shape
    H, d = CA_HEADS, CA_HEAD_DIM
    n_chunks = seq // CHUNK
    band_chunks = CA_LEFT_CHUNKS + 1
    qc = q.reshape(bsz, n_chunks, CHUNK, H, d)

    def band(t):
        tc = t.reshape(bsz, n_chunks, CHUNK, H, d)
        tp = jnp.pad(tc, ((0, 0), (CA_LEFT_CHUNKS, 0), (0, 0), (0, 0), (0, 0)))
        return jnp.concatenate([tp[:, o:o + n_chunks] for o in range(band_chunks)], axis=2)

    kb, vb = band(k), band(v)
    s = jnp.einsum('bnihd,bnjhd->bhnij', qc, kb).astype(f32) * (d ** -0.5)
    qi = jnp.arange(CHUNK)
    kj = jnp.arange(band_chunks * CHUNK)
    dist = qi[:, None] + CA_LEFT_CHUNKS * CHUNK - kj[None, :]
    bias_idx = jnp.clip(dist, -MAX_REL_DIST, MAX_REL_DIST) + MAX_REL_DIST
    bias = rel_bias.astype(f32)[:, bias_idx]
    valid = (jnp.arange(n_chunks)[:, None] - CA_LEFT_CHUNKS + kj[None, :] // CHUNK) >= 0
    s = jnp.where(valid[None, None, :, None, :], s + bias[None, :, None], -jnp.inf)
    p = jax.nn.softmax(s, axis=-1).astype(v.dtype)
    o = jnp.einsum('bhnij,bnjhd->bnihd', p, vb).reshape(bsz, seq, H * d)
    return rmsnorm(o, norm_w)


def moe(h, w_router, b_router, w_gate, b_gate, w_up, b_up, w_down, b_down):
    bsz, seq, dm = h.shape
    n_tok = bsz * seq
    xf = h.reshape(n_tok, dm)
    logits = (xf @ w_router).astype(jnp.float32) + b_router.astype(jnp.float32)
    top_v, top_i = lax.top_k(logits, TOP_K)
    gates = jax.nn.softmax(top_v, axis=-1)
    n_assign = n_tok * TOP_K
    flat_e = top_i.reshape(-1)
    flat_tok = jnp.repeat(jnp.arange(n_tok, dtype=jnp.int32), TOP_K)
    flat_gate = gates.reshape(-1)
    order = jnp.argsort(flat_e, stable=True)
    sorted_e = flat_e[order]
    counts = jnp.bincount(flat_e, length=N_EXPERTS)
    padded = (counts + EXPERT_BLOCK - 1) // EXPERT_BLOCK * EXPERT_BLOCK
    start = jnp.cumsum(counts) - counts
    pad_end = jnp.cumsum(padded)
    pad_start = pad_end - padded
    dest = pad_start[sorted_e] + (jnp.arange(n_assign) - start[sorted_e])
    n_blocks = -(-n_assign // EXPERT_BLOCK) + N_EXPERTS
    n_slots = n_blocks * EXPERT_BLOCK
    slot_tok = jnp.zeros((n_slots,), jnp.int32).at[dest].set(flat_tok[order])
    slot_gate = jnp.zeros((n_slots,), jnp.float32).at[dest].set(flat_gate[order])
    block_e = jnp.minimum(
        jnp.searchsorted(pad_end, jnp.arange(n_blocks) * EXPERT_BLOCK, side='right'), N_EXPERTS - 1)
    xs = xf[slot_tok].reshape(n_blocks, EXPERT_BLOCK, dm)

    def expert_fn(args):
        xb, e = args
        gate = xb @ w_gate[e] + b_gate[e]
        up = xb @ w_up[e] + b_up[e]
        gate = jnp.minimum(gate, SWIGLU_LIMIT)
        up = jnp.clip(up, -SWIGLU_LIMIT, SWIGLU_LIMIT)
        glu = gate * jax.nn.sigmoid(gate * SWIGLU_ALPHA)
        return ((up + 1.0) * glu) @ w_down[e] + b_down[e]

    ys = lax.map(expert_fn, (xs, block_e)).reshape(n_slots, dm).astype(jnp.float32)
    out = jnp.zeros((n_tok, dm), jnp.float32).at[slot_tok].add(ys * slot_gate[:, None])
    return out.reshape(bsz, seq, dm).astype(h.dtype)


def hybrid_layer(x, norm1_w, w_in, conv_w, a_log, dt_bias, dn_norm_w, rel_bias, attn_norm_w,
                 w_out, norm2_w, w_router, b_router, w_gate, b_gate, w_up, b_up, w_down, b_down):
    h = rmsnorm(x, norm1_w)
    proj = h @ w_in
    split_at = np.cumsum(IN_SPLITS)[:-1].tolist()
    q_a, k_a, v_a, z_a, b_a, a_a, q_b, k_b, v_b = jnp.split(proj, split_at, axis=-1)
    qkv_a = jax.nn.silu(causal_depthwise_conv(jnp.concatenate([q_a, k_a, v_a], axis=-1), conv_w))
    q_a, k_a, v_a = jnp.split(qkv_a, 3, axis=-1)
    o_a = gated_deltanet(q_a, k_a, v_a, z_a, b_a, a_a, a_log, dt_bias, dn_norm_w)
    o_b = chunk_band_attention(q_b, k_b, v_b, rel_bias, attn_norm_w)
    x = x + jnp.concatenate([o_a, o_b], axis=-1) @ w_out
    x = x + moe(rmsnorm(x, norm2_w), w_router, b_router, w_gate, b_gate, w_up, b_up, w_down, b_down)
    return x


def setup_inputs(seed: int = 0) -> dict:
    key = jax.random.key(seed)
    ks = jax.random.split(key, 20)
    L, D, E, F = DEPTH, D_MODEL, N_EXPERTS, D_EXPERT

    def nrm(k, shape, scale):
        return jax.random.normal(k, shape, jnp.float32) * scale

    def gain(k, shape):
        return 1.0 + 0.02 * jax.random.normal(k, shape, jnp.float32)

    x = nrm(ks[0], (BATCH, SEQ, D), 1.0)
    norm1_w = gain(ks[1], (L, D))
    w_in = nrm(ks[2], (L, D, IN_WIDTH), D ** -0.5)
    conv_w = nrm(ks[3], (L, CONV_WIDTH, 3 * DN_WIDTH), CONV_WIDTH ** -0.5)
    a_log = jnp.log(jax.random.uniform(ks[4], (L, DN_HEADS), jnp.float32, 1.0, 16.0))
    dt = jnp.exp(jax.random.uniform(ks[5], (L, DN_HEADS), jnp.float32, math.log(1e-3), math.log(1e-1)))
    dt_bias = dt + jnp.log(-jnp.expm1(-dt))
    dn_norm_w = gain(ks[6], (L, DN_HEAD_DIM))
    rel_bias = nrm(ks[7], (L, CA_HEADS, 2 * MAX_REL_DIST + 1), 0.5)
    attn_norm_w = gain(ks[8], (L, CA_WIDTH))
    w_out = nrm(ks[9], (L, MIX_WIDTH, D), MIX_WIDTH ** -0.5)
    norm2_w = gain(ks[10], (L, D))
    w_router = nrm(ks[11], (L, D, E), D ** -0.5)
    b_router = nrm(ks[12], (L, E), 0.01)
    w_gate = nrm(ks[13], (L, E, D, F), D ** -0.5)
    b_gate = nrm(ks[14], (L, E, F), 0.01)
    w_up = nrm(ks[15], (L, E, D, F), D ** -0.5)
    b_up = nrm(ks[16], (L, E, F), 0.01)
    w_down = nrm(ks[17], (L, E, F, D), F ** -0.5)
    b_down = nrm(ks[18], (L, E, D), 0.01)
    final_norm_w = gain(ks[19], (D,))
    return {'x': x, 'norm1_w': norm1_w, 'w_in': w_in, 'conv_w': conv_w, 'a_log': a_log,
            'dt_bias': dt_bias, 'dn_norm_w': dn_norm_w, 'rel_bias': rel_bias,
            'attn_norm_w': attn_norm_w, 'w_out': w_out, 'norm2_w': norm2_w,
            'w_router': w_router, 'b_router': b_router, 'w_gate': w_gate, 'b_gate': b_gate,
            'w_up': w_up, 'b_up': b_up, 'w_down': w_down, 'b_down': b_down,
            'final_norm_w': final_norm_w}


def reference(x, norm1_w, w_in, conv_w, a_log, dt_bias, dn_norm_w, rel_bias, attn_norm_w,
              w_out, norm2_w, w_router, b_router, w_gate, b_gate, w_up, b_up, w_down, b_down,
              final_norm_w):
    for l in range(DEPTH):
        x = hybrid_layer(x, norm1_w[l], w_in[l], conv_w[l], a_log[l], dt_bias[l], dn_norm_w[l],
                         rel_bias[l], attn_norm_w[l], w_out[l], norm2_w[l], w_router[l],
                         b_router[l], w_gate[l], b_gate[l], w_up[l], b_up[l], w_down[l], b_down[l])
    return rmsnorm(x, final_norm_w)
```

```python
import functools
import math

import jax
import jax.numpy as jnp
from jax import lax
from jax.experimental import pallas as pl
from jax.experimental.pallas import tpu as pltpu

F32 = jnp.float32
BF16 = jnp.bfloat16
HIGHEST = lax.Precision.HIGHEST

NORM_EPS = 1e-6
CHUNK = 64
DN_HEADS = 4
DN_HEAD_DIM = 128
DN_WIDTH = DN_HEADS * DN_HEAD_DIM
CONV_WIDTH = 4
CA_HEADS = 8
CA_HEAD_DIM = 64
CA_WIDTH = CA_HEADS * CA_HEAD_DIM
CA_LEFT_CHUNKS = 8
BAND = (CA_LEFT_CHUNKS + 1) * CHUNK
MAX_REL_DIST = 256
N_EXPERTS = 32
TOP_K = 4
SWIGLU_LIMIT = 7.0
SWIGLU_ALPHA = 1.702

LANES = 128
INV_BLOCK = 16
EXPERT_BLOCK_ROWS = 256
VMEM_LIMIT = 48 * 1024 * 1024


def _dot(a, b, dims=(((1,), (0,)), ((), ())), precision=None):
    return lax.dot_general(a, b, dims, precision=precision, preferred_element_type=F32)


def _dot_nt(a, b, precision=None):
    return _dot(a, b, (((1,), (1,)), ((), ())), precision)


def _sigmoid(x):
    return 1.0 / (1.0 + jnp.exp(-x))


def _silu(x):
    return x * _sigmoid(x)


def _in_proj_kernel(x_ref, nw_ref, w_ref, qkva_ref, z_ref, ba_ref, qb_ref, kb_ref, vb_ref):
    x = x_ref[...]
    h = x * lax.rsqrt(jnp.mean(x * x, axis=-1, keepdims=True) + NORM_EPS) * nw_ref[...]
    hb = h.astype(BF16)
    c0 = 3 * DN_WIDTH
    c1 = c0 + DN_WIDTH
    c2 = c1 + CA_WIDTH
    c3 = c2 + CA_WIDTH
    c4 = c3 + CA_WIDTH
    qkva_ref[...] = _dot(hb, w_ref[:, 0:c0])
    z_ref[...] = _dot(hb, w_ref[:, c0:c1])
    qb_ref[...] = _dot(hb, w_ref[:, c1:c2]).astype(BF16)
    kb_ref[...] = _dot(hb, w_ref[:, c2:c3]).astype(BF16)
    vb_ref[...] = _dot(hb, w_ref[:, c3:c4]).astype(BF16)
    ba_ref[...] = _dot(hb, w_ref[:, c4:c4 + LANES])


def _in_proj(x2, norm_w, w_all, tm):
    t, d = x2.shape
    wcols = w_all.shape[1]
    row = lambda i: (i, 0)
    fixed = lambda i: (0, 0)
    return pl.pallas_call(
        _in_proj_kernel,
        grid=(t // tm,),
        in_specs=[
            pl.BlockSpec((tm, d), row),
            pl.BlockSpec((1, d), fixed),
            pl.BlockSpec((d, wcols), fixed),
        ],
        out_specs=[
            pl.BlockSpec((tm, 3 * DN_WIDTH), row),
            pl.BlockSpec((tm, DN_WIDTH), row),
            pl.BlockSpec((tm, LANES), row),
            pl.BlockSpec((tm, CA_WIDTH), row),
            pl.BlockSpec((tm, CA_WIDTH), row),
            pl.BlockSpec((tm, CA_WIDTH), row),
        ],
        out_shape=[
            jax.ShapeDtypeStruct((t, 3 * DN_WIDTH), F32),
            jax.ShapeDtypeStruct((t, DN_WIDTH), F32),
            jax.ShapeDtypeStruct((t, LANES), F32),
            jax.ShapeDtypeStruct((t, CA_WIDTH), BF16),
            jax.ShapeDtypeStruct((t, CA_WIDTH), BF16),
            jax.ShapeDtypeStruct((t, CA_WIDTH), BF16),
        ],
        compiler_params=pltpu.CompilerParams(
            dimension_semantics=("parallel",), vmem_limit_bytes=VMEM_LIMIT),
        name="in_proj",
    )(x2, norm_w, w_all)


def _unit_lower_inverse(lower, rows, cols):
    eye = (rows == cols).astype(F32)
    same_block = (rows // INV_BLOCK) == (cols // INV_BLOCK)
    diag = jnp.where(same_block, lower, 0.0)
    off = lower - diag
    mm = functools.partial(_dot, precision=HIGHEST)
    d2 = mm(diag, diag)
    d4 = mm(d2, d2)
    d8 = mm(d4, d4)
    p = mm(mm(eye - diag, eye + d2), mm(eye + d4, eye + d8))
    m = mm(p, off)
    m2 = mm(m, m)
    return mm(mm(eye - m, eye + m2), p)


def _deltanet_kernel(cur_ref, halo_ref, z_ref, ba_ref, convw_ref, alog_ref, dtb_ref,
                     normw_ref, o_ref, state_ref):
    n = pl.program_id(1)

    @pl.when(n == 0)
    def _():
        state_ref[...] = jnp.zeros_like(state_ref)

    cur = cur_ref[0]
    halo = jnp.where(n > 0, halo_ref[0], 0.0)
    full = jnp.concatenate([halo, cur], axis=0)
    conv = full[8:] * convw_ref[CONV_WIDTH - 1:CONV_WIDTH, :]
    for j in range(CONV_WIDTH - 1):
        shift = CONV_WIDTH - 1 - j
        conv = conv + pltpu.roll(full, shift, axis=0)[8:] * convw_ref[j:j + 1, :]
    qkv = _silu(conv)

    ba = ba_ref[0]
    beta_all = _sigmoid(ba)
    sp_in = ba + dtb_ref[...]
    softplus = jnp.maximum(sp_in, 0.0) + jnp.log1p(jnp.exp(-jnp.abs(sp_in)))
    g_all = -jnp.exp(alog_ref[...]) * softplus
    rows = lax.broadcasted_iota(jnp.int32, (CHUNK, CHUNK), 0)
    cols = lax.broadcasted_iota(jnp.int32, (CHUNK, CHUNK), 1)
    causal = rows >= cols
    strict = rows > cols
    gc_all = _dot(causal.astype(F32), g_all, precision=HIGHEST)
    gc_rows = gc_all.T

    scale = DN_HEAD_DIM ** -0.5
    for h in range(DN_HEADS):
        lo = h * DN_HEAD_DIM
        q = qkv[:, lo:lo + DN_HEAD_DIM]
        k = qkv[:, DN_WIDTH + lo:DN_WIDTH + lo + DN_HEAD_DIM]
        v = qkv[:, 2 * DN_WIDTH + lo:2 * DN_WIDTH + lo + DN_HEAD_DIM]
        q = q * lax.rsqrt(jnp.sum(q * q, axis=-1, keepdims=True) + NORM_EPS) * scale
        k = k * lax.rsqrt(jnp.sum(k * k, axis=-1, keepdims=True) + NORM_EPS)
        beta = beta_all[:, h:h + 1]
        gc = gc_all[:, DN_HEADS + h:DN_HEADS + h + 1]
        gc_row = gc_rows[DN_HEADS + h:DN_HEADS + h + 1, :]
        gc_last = gc_all[CHUNK - 1:CHUNK, DN_HEADS + h:DN_HEADS + h + 1]
        diff = gc - gc_row
        decay = jnp.where(causal, jnp.exp(jnp.where(causal, diff, 0.0)), 0.0)
        egc = jnp.exp(gc)
        k_beta = k * beta
        kb16 = k.astype(BF16)
        lower = jnp.where(strict, _dot_nt(k_beta.astype(BF16), kb16) * decay, 0.0)
        t_inv = _unit_lower_inverse(lower, rows, cols)
        rhs = jnp.concatenate([v * beta, k_beta * egc], axis=-1)
        sol = _dot(t_inv, rhs, precision=HIGHEST)
        u = sol[:, :DN_HEAD_DIM]
        w = sol[:, DN_HEAD_DIM:]
        qk = _dot_nt(q.astype(BF16), kb16) * decay
        q_dec = q * egc
        k_dec = k * jnp.exp(gc_last - gc)

        state = state_ref[h]
        s16 = state.astype(BF16)
        v_new = u - _dot(w.astype(BF16), s16)
        o = _dot(q_dec.astype(BF16), s16) + _dot(qk.astype(BF16), v_new.astype(BF16))
        state_ref[h] = state * jnp.exp(gc_last) + _dot(k_dec.T.astype(BF16), v_new.astype(BF16))

        o = o * lax.rsqrt(jnp.mean(o * o, axis=-1, keepdims=True) + NORM_EPS) * normw_ref[...]
        o_ref[0, :, lo:lo + DN_HEAD_DIM] = o * _silu(z_ref[0, :, lo:lo + DN_HEAD_DIM])


def _deltanet(qkva, z, ba, conv_w, alog_row, dtb_row, norm_w):
    b, s, _ = qkva.shape
    n_chunks = s // CHUNK
    halo_blocks = CHUNK // 8
    fixed = lambda i, n: (0, 0)
    return pl.pallas_call(
        _deltanet_kernel,
        grid=(b, n_chunks),
        in_specs=[
            pl.BlockSpec((1, CHUNK, 3 * DN_WIDTH), lambda i, n: (i, n, 0)),
            pl.BlockSpec((1, 8, 3 * DN_WIDTH),
                         lambda i, n: (i, jnp.maximum(n * halo_blocks - 1, 0), 0)),
            pl.BlockSpec((1, CHUNK, DN_WIDTH), lambda i, n: (i, n, 0)),
            pl.BlockSpec((1, CHUNK, LANES), lambda i, n: (i, n, 0)),
            pl.BlockSpec((CONV_WIDTH, 3 * DN_WIDTH), fixed),
            pl.BlockSpec((1, LANES), fixed),
            pl.BlockSpec((1, LANES), fixed),
            pl.BlockSpec((1, DN_HEAD_DIM), fixed),
        ],
        out_specs=pl.BlockSpec((1, CHUNK, DN_WIDTH), lambda i, n: (i, n, 0)),
        out_shape=jax.ShapeDtypeStruct((b, s, DN_WIDTH), F32),
        scratch_shapes=[pltpu.VMEM((DN_HEADS, DN_HEAD_DIM, DN_HEAD_DIM), F32)],
        compiler_params=pltpu.CompilerParams(
            dimension_semantics=("parallel", "arbitrary"), vmem_limit_bytes=VMEM_LIMIT),
        name="deltanet",
    )(qkva, qkva, z, ba, conv_w, alog_row, dtb_row, norm_w)


def _band_attn_kernel(q_ref, k_ref, v_ref, bias_ref, nw_ref, o_ref):
    n = pl.program_id(1)
    band_chunks = CA_LEFT_CHUNKS + 1
    k_parts, v_parts = [], []
    for w in range(band_chunks):
        start = pl.multiple_of(jnp.maximum(n - CA_LEFT_CHUNKS + w, 0) * CHUNK, CHUNK)
        k_parts.append(k_ref[0, pl.ds(start, CHUNK), :])
        v_parts.append(v_ref[0, pl.ds(start, CHUNK), :])
    k_win = jnp.concatenate(k_parts, axis=0)
    v_win = jnp.concatenate(v_parts, axis=0)
    q = q_ref[0]

    key_pos = lax.broadcasted_iota(jnp.int32, (CHUNK, BAND), 1)
    valid = key_pos >= (CA_LEFT_CHUNKS - n) * CHUNK
    lane = lax.broadcasted_iota(jnp.int32, (CHUNK, LANES), 1)
    scale = CA_HEAD_DIM ** -0.5
    heads_per_group = LANES // CA_HEAD_DIM

    outs = []
    for grp in range(CA_WIDTH // LANES):
        lo = grp * LANES
        qg = q[:, lo:lo + LANES]
        kg = k_win[:, lo:lo + LANES]
        vg = v_win[:, lo:lo + LANES]
        og = jnp.zeros((CHUNK, LANES), F32)
        for hh in range(heads_per_group):
            head = grp * heads_per_group + hh
            in_head = (lane // CA_HEAD_DIM) == hh
            qm = jnp.where(in_head, qg, jnp.zeros_like(qg))
            s = _dot_nt(qm, kg) * scale + bias_ref[head]
            s = jnp.where(valid, s, -jnp.inf)
            e = jnp.exp(s - jnp.max(s, axis=-1, keepdims=True))
            p = e / jnp.sum(e, axis=-1, keepdims=True)
            og = jnp.where(in_head, _dot(p.astype(BF16), vg), og)
        outs.append(og)
    o = jnp.concatenate(outs, axis=-1)
    o = o * lax.rsqrt(jnp.mean(o * o, axis=-1, keepdims=True) + NORM_EPS) * nw_ref[...]
    o_ref[0] = o


def _band_attn(qb, kb, vb, bias, norm_w):
    b, s, _ = qb.shape
    n_chunks = s // CHUNK
    return pl.pallas_call(
        _band_attn_kernel,
        grid=(b, n_chunks),
        in_specs=[
            pl.BlockSpec((1, CHUNK, CA_WIDTH), lambda i, n: (i, n, 0)),
            pl.BlockSpec((1, s, CA_WIDTH), lambda i, n: (i, 0, 0)),
            pl.BlockSpec((1, s, CA_WIDTH), lambda i, n: (i, 0, 0)),
            pl.BlockSpec((CA_HEADS, CHUNK, BAND), lambda i, n: (0, 0, 0)),
            pl.BlockSpec((1, CA_WIDTH), lambda i, n: (0, 0)),
        ],
        out_specs=pl.BlockSpec((1, CHUNK, CA_WIDTH), lambda i, n: (i, n, 0)),
        out_shape=jax.ShapeDtypeStruct((b, s, CA_WIDTH), F32),
        compiler_params=pltpu.CompilerParams(
            dimension_semantics=("parallel", "arbitrary"), vmem_limit_bytes=VMEM_LIMIT),
        name="band_attn",
    )(qb, kb, vb, bias, norm_w)


def _out_router_kernel(oa_ref, ob_ref, x_ref, wout_ref, n2_ref, wr_ref, br_ref,
                       x1_ref, h2_ref, route_ref, gates_ref, counts_ref, run_ref):
    i = pl.program_id(0)

    @pl.when(i == 0)
    def _():
        run_ref[...] = jnp.zeros_like(run_ref)

    x1 = (x_ref[...]
          + _dot(oa_ref[...].astype(BF16), wout_ref[0:DN_WIDTH, :])
          + _dot(ob_ref[...].astype(BF16), wout_ref[DN_WIDTH:, :]))
    x1_ref[...] = x1
    h2 = x1 * lax.rsqrt(jnp.mean(x1 * x1, axis=-1, keepdims=True) + NORM_EPS) * n2_ref[...]
    h2_ref[...] = h2

    tm = x1.shape[0]
    lane = lax.broadcasted_iota(jnp.int32, (tm, LANES), 1)
    logits = _dot(h2, wr_ref[...], precision=HIGHEST) + br_ref[...]
    logits = jnp.where(lane < N_EXPERTS, logits, -jnp.inf)

    top_vals, top_idx, onehots = [], [], []
    for _ in range(TOP_K):
        m = jnp.max(logits, axis=-1, keepdims=True)
        idx = jnp.min(jnp.where(logits == m, lane, LANES), axis=-1, keepdims=True)
        hit = lane == idx
        top_vals.append(m)
        top_idx.append(idx)
        onehots.append(hit)
        logits = jnp.where(hit, -jnp.inf, logits)

    exps = [jnp.exp(v - top_vals[0]) for v in top_vals]
    denom = exps[0] + exps[1] + exps[2] + exps[3]
    gates = jnp.zeros((tm, LANES), F32)
    for k in range(TOP_K):
        gates = jnp.where(lane == k, exps[k] / denom, gates)
    gates_ref[...] = gates

    multi = jnp.zeros((tm, LANES), F32)
    for hit in onehots:
        multi = multi + hit.astype(F32)
    r = lax.broadcasted_iota(jnp.int32, (tm, tm), 0)
    c = lax.broadcasted_iota(jnp.int32, (tm, tm), 1)
    before = _dot((r > c).astype(BF16), multi.astype(BF16)) + run_ref[...]
    route = jnp.zeros((tm, LANES), jnp.int32)
    for k in range(TOP_K):
        rank = jnp.sum(jnp.where(onehots[k], before, 0.0), axis=-1, keepdims=True).astype(jnp.int32)
        route = jnp.where(lane == k, top_idx[k], route)
        route = jnp.where(lane == TOP_K + k, rank, route)
    route_ref[...] = route
    run_ref[...] = run_ref[...] + jnp.sum(multi, axis=0, keepdims=True)
    counts_ref[...] = run_ref[...]


def _out_router(oa, ob, x2, w_out, norm2_w, wr_pad, br_pad, tm):
    t, d = x2.shape
    row = lambda i: (i, 0)
    fixed = lambda i: (0, 0)
    return pl.pallas_call(
        _out_router_kernel,
        grid=(t // tm,),
        in_specs=[
            pl.BlockSpec((tm, DN_WIDTH), row),
            pl.BlockSpec((tm, CA_WIDTH), row),
            pl.BlockSpec((tm, d), row),
            pl.BlockSpec((DN_WIDTH + CA_WIDTH, d), fixed),
            pl.BlockSpec((1, d), fixed),
            pl.BlockSpec((d, LANES), fixed),
            pl.BlockSpec((1, LANES), fixed),
        ],
        out_specs=[
            pl.BlockSpec((tm, d), row),
            pl.BlockSpec((tm, d), row),
            pl.BlockSpec((tm, LANES), row),
            pl.BlockSpec((tm, LANES), row),
            pl.BlockSpec((1, LANES), fixed),
        ],
        out_shape=[
            jax.ShapeDtypeStruct((t, d), F32),
            jax.ShapeDtypeStruct((t, d), F32),
            jax.ShapeDtypeStruct((t, LANES), jnp.int32),
            jax.ShapeDtypeStruct((t, LANES), F32),
            jax.ShapeDtypeStruct((1, LANES), F32),
        ],
        scratch_shapes=[pltpu.VMEM((1, LANES), F32)],
        compiler_params=pltpu.CompilerParams(
            dimension_semantics=("arbitrary",), vmem_limit_bytes=VMEM_LIMIT),
        name="out_router",
    )(oa, ob, x2, w_out, norm2_w, wr_pad, br_pad)


def _gather_rows_kernel(idx_ref, src_ref, o_ref, sem):
    rows = o_ref.shape[0]

    def row_copy(r, tok):
        return pltpu.make_async_copy(src_ref.at[pl.ds(tok, 1)], o_ref.at[pl.ds(r, 1)], sem)

    def issue(r, carry):
        row_copy(r, idx_ref[0, 0, r]).start()
        return carry

    lax.fori_loop(0, rows, issue, 0)

    def drain(r, carry):
        row_copy(r, 0).wait()
        return carry

    lax.fori_loop(0, rows, drain, 0)


def _gather_rows(src, idx, rows_per_block):
    n = idx.shape[0]
    d = src.shape[1]
    n_blocks = n // rows_per_block
    return pl.pallas_call(
        _gather_rows_kernel,
        grid=(n_blocks,),
        in_specs=[
            pl.BlockSpec((1, 1, rows_per_block), lambda b: (b, 0, 0), memory_space=pltpu.SMEM),
            pl.BlockSpec(memory_space=pl.ANY),
        ],
        out_specs=pl.BlockSpec((rows_per_block, d), lambda b: (b, 0)),
        out_shape=jax.ShapeDtypeStruct((n, d), src.dtype),
        scratch_shapes=[pltpu.SemaphoreType.DMA],
        compiler_params=pltpu.CompilerParams(dimension_semantics=("arbitrary",)),
        name="gather_rows",
    )(idx.reshape(n_blocks, 1, rows_per_block), src)


def _experts_kernel(be_ref, nused_ref, xs_ref, wg_ref, bg_ref, wu_ref, bu_ref, wd_ref, bd_ref,
                    ys_ref):
    blk = pl.program_id(0)

    @pl.when(blk < nused_ref[0])
    def _():
        x = xs_ref[...].astype(BF16)
        gate = _dot(x, wg_ref[0]) + bg_ref[0]
        up = _dot(x, wu_ref[0]) + bu_ref[0]
        gate = jnp.minimum(gate, SWIGLU_LIMIT)
        up = jnp.clip(up, -SWIGLU_LIMIT, SWIGLU_LIMIT)
        glu = gate * _sigmoid(gate * SWIGLU_ALPHA)
        mid = ((up + 1.0) * glu).astype(BF16)
        ys_ref[...] = _dot(mid, wd_ref[0]) + bd_ref[0]

    @pl.when(blk >= nused_ref[0])
    def _():
        ys_ref[...] = jnp.zeros_like(ys_ref)


def _experts(xs, block_e, n_used, wg, bg, wu, bu, wd, bd, bm):
    n, d = xs.shape
    f = wg.shape[2]
    n_blocks = n // bm
    wmap = lambda b, be, nu: (be[b], 0, 0)
    grid_spec = pltpu.PrefetchScalarGridSpec(
        num_scalar_prefetch=2,
        grid=(n_blocks,),
        in_specs=[
            pl.BlockSpec((bm, d), lambda b, be, nu: (b, 0)),
            pl.BlockSpec((1, d, f), wmap),
            pl.BlockSpec((1, 1, f), wmap),
            pl.BlockSpec((1, d, f), wmap),
            pl.BlockSpec((1, 1, f), wmap),
            pl.BlockSpec((1, f, d), wmap),
            pl.BlockSpec((1, 1, d), wmap),
        ],
        out_specs=pl.BlockSpec((bm, d), lambda b, be, nu: (b, 0)),
    )
    return pl.pallas_call(
        _experts_kernel,
        grid_spec=grid_spec,
        out_shape=jax.ShapeDtypeStruct((n, d), F32),
        compiler_params=pltpu.CompilerParams(
            dimension_semantics=("arbitrary",), vmem_limit_bytes=VMEM_LIMIT),
        name="experts",
    )(block_e, n_used, xs, wg, bg, wu, bu, wd, bd)


def _combine_kernel(pos_ref, ys_ref, x1_ref, gates_ref, fw_ref, o_ref, buf_ref, sem):
    tm = x1_ref.shape[0]

    def row_copy(r, k, slot):
        return pltpu.make_async_copy(ys_ref.at[pl.ds(slot, 1)], buf_ref.at[k, pl.ds(r, 1)], sem)

    def issue(r, carry):
        for k in range(TOP_K):
            row_copy(r, k, pos_ref[0, 0, r * TOP_K + k]).start()
        return carry

    lax.fori_loop(0, tm, issue, 0)

    def drain(r, carry):
        for k in range(TOP_K):
            row_copy(r, k, 0).wait()
        return carry

    lax.fori_loop(0, tm, drain, 0)

    acc = x1_ref[...]
    gates = gates_ref[...]
    for k in range(TOP_K):
        acc = acc + buf_ref[k] * gates[:, k:k + 1]
    o_ref[...] = acc * lax.rsqrt(jnp.mean(acc * acc, axis=-1, keepdims=True) + NORM_EPS) * fw_ref[...]


def _combine(pos, ys, x1, gates, final_w, tm):
    t, d = x1.shape
    n_tiles = t // tm
    row = lambda i: (i, 0)
    return pl.pallas_call(
        _combine_kernel,
        grid=(n_tiles,),
        in_specs=[
            pl.BlockSpec((1, 1, tm * TOP_K), lambda i: (i, 0, 0), memory_space=pltpu.SMEM),
            pl.BlockSpec(memory_space=pl.ANY),
            pl.BlockSpec((tm, d), row),
            pl.BlockSpec((tm, LANES), row),
            pl.BlockSpec((1, d), lambda i: (0, 0)),
        ],
        out_specs=pl.BlockSpec((tm, d), row),
        out_shape=jax.ShapeDtypeStruct((t, d), F32),
        scratch_shapes=[pltpu.VMEM((TOP_K, tm, d), F32), pltpu.SemaphoreType.DMA],
        compiler_params=pltpu.CompilerParams(
            dimension_semantics=("arbitrary",), vmem_limit_bytes=VMEM_LIMIT),
        name="combine",
    )(pos.reshape(n_tiles, 1, tm * TOP_K), ys, x1, gates, final_w)


def _rel_bias_table(rel_bias):
    qi = jnp.arange(CHUNK)
    kj = jnp.arange(BAND)
    dist = qi[:, None] + CA_LEFT_CHUNKS * CHUNK - kj[None, :]
    idx = jnp.clip(dist, -MAX_REL_DIST, MAX_REL_DIST) + MAX_REL_DIST
    return rel_bias.astype(F32)[:, idx]


def _layer(x, norm1_w, w_in, conv_w, a_log, dt_bias, dn_norm_w, rel_bias, attn_norm_w, w_out,
           norm2_w, w_router, b_router, w_gate, b_gate, w_up, b_up, w_down, b_down, out_norm_w):
    b, s, d = x.shape
    t = b * s
    x2 = x.reshape(t, d)

    ba_lo = 4 * DN_WIDTH
    ba_hi = ba_lo + 2 * DN_HEADS
    w_small = jnp.pad(w_in[:, ba_lo:ba_hi], ((0, 0), (0, LANES - 2 * DN_HEADS)))
    w_all = jnp.concatenate([w_in[:, :ba_lo], w_in[:, ba_hi:], w_small], axis=1).astype(BF16)
    qkva, z_a, ba, q_b, k_b, v_b = _in_proj(x2, norm1_w.reshape(1, d), w_all, tm=512)

    lane_pad = (DN_HEADS, LANES - 2 * DN_HEADS)
    alog_row = jnp.pad(a_log.astype(F32), lane_pad).reshape(1, LANES)
    dtb_row = jnp.pad(dt_bias.astype(F32), lane_pad).reshape(1, LANES)
    o_a = _deltanet(qkva.reshape(b, s, 3 * DN_WIDTH), z_a.reshape(b, s, DN_WIDTH),
                    ba.reshape(b, s, LANES), conv_w, alog_row, dtb_row,
                    dn_norm_w.reshape(1, DN_HEAD_DIM))
    o_b = _band_attn(q_b.reshape(b, s, CA_WIDTH), k_b.reshape(b, s, CA_WIDTH),
                     v_b.reshape(b, s, CA_WIDTH), _rel_bias_table(rel_bias),
                     attn_norm_w.reshape(1, CA_WIDTH))

    wr_pad = jnp.pad(w_router.astype(F32), ((0, 0), (0, LANES - N_EXPERTS)))
    br_pad = jnp.pad(b_router.astype(F32), (0, LANES - N_EXPERTS)).reshape(1, LANES)
    x1, h2, route, gates, counts = _out_router(
        o_a.reshape(t, DN_WIDTH), o_b.reshape(t, CA_WIDTH), x2, w_out.astype(BF16),
        norm2_w.reshape(1, d), wr_pad, br_pad, tm=512)

    bm = EXPERT_BLOCK_ROWS
    n_assign = t * TOP_K
    n_blocks = n_assign // bm + N_EXPERTS
    n_slots = n_blocks * bm
    counts_i = counts[0, :N_EXPERTS].astype(jnp.int32)
    padded = (counts_i + bm - 1) // bm * bm
    pad_end = jnp.cumsum(padded)
    pad_start = pad_end - padded
    top_e = route[:, :TOP_K]
    pos = pad_start[top_e] + route[:, TOP_K:2 * TOP_K]
    tok = jnp.broadcast_to(jnp.arange(t, dtype=jnp.int32)[:, None], (t, TOP_K))
    slot_tok = jnp.zeros((n_slots,), jnp.int32).at[pos.reshape(-1)].set(tok.reshape(-1))
    block_e = jnp.minimum(
        jnp.searchsorted(pad_end, jnp.arange(n_blocks, dtype=jnp.int32) * bm, side='right'),
        N_EXPERTS - 1).astype(jnp.int32)
    n_used = (pad_end[-1:] // bm).astype(jnp.int32)

    xs = _gather_rows(h2, slot_tok, bm)
    f = w_gate.shape[-1]
    ys = _experts(xs, block_e, n_used,
                  w_gate.astype(BF16), b_gate.reshape(N_EXPERTS, 1, f),
                  w_up.astype(BF16), b_up.reshape(N_EXPERTS, 1, f),
                  w_down.astype(BF16), b_down.reshape(N_EXPERTS, 1, d), bm)
    out = _combine(pos.reshape(-1), ys, x1, gates, out_norm_w.reshape(1, d), tm=256)
    return out.reshape(b, s, d)


def kernel(x, norm1_w, w_in, conv_w, a_log, dt_bias, dn_norm_w, rel_bias, attn_norm_w, w_out, norm2_w, w_router, b_router, w_gate, b_gate, w_up, b_up, w_down, b_down, final_norm_w):
    depth = norm1_w.shape[0]
    assert depth == 1, "the final RMSNorm is fused into the last layer's combine step"
    return _layer(x, norm1_w[0], w_in[0], conv_w[0], a_log[0], dt_bias[0], dn_norm_w[0],
                  rel_bias[0], attn_norm_w[0], w_out[0], norm2_w[0], w_router[0], b_router[0],
                  w_gate[0], b_gate[0], w_up[0], b_up[0], w_down[0], b_down[0], final_norm_w)
```

```python
import functools

import jax
import jax.numpy as jnp
from jax import lax
from jax.experimental import pallas as pl
from jax.experimental.pallas import tpu as pltpu

F32 = jnp.float32
BF16 = jnp.bfloat16
HIGHEST = lax.Precision.HIGHEST

NORM_EPS = 1e-6
CHUNK = 64
DN_HEADS = 4
DN_HEAD_DIM = 128
DN_WIDTH = DN_HEADS * DN_HEAD_DIM
CONV_WIDTH = 4
CA_HEADS = 8
CA_HEAD_DIM = 64
CA_WIDTH = CA_HEADS * CA_HEAD_DIM
CA_LEFT_CHUNKS = 8
BAND = (CA_LEFT_CHUNKS + 1) * CHUNK
MAX_REL_DIST = 256
N_EXPERTS = 32
TOP_K = 4
SWIGLU_LIMIT = 7.0
SWIGLU_ALPHA = 1.702

LANES = 128
SUBLANES = 8
INV_BLOCK = 16
PREP_ROWS = 256
EXPERT_BLOCK_ROWS = 256
VMEM_LIMIT = 48 * 1024 * 1024


def _dot(a, b, dims=(((1,), (0,)), ((), ())), precision=None):
    return lax.dot_general(a, b, dims, precision=precision, preferred_element_type=F32)


def _dot_nt(a, b, precision=None):
    return _dot(a, b, (((1,), (1,)), ((), ())), precision)


def _bdot(a, b):
    return _dot(a.astype(BF16), b.astype(BF16))


def _sigmoid(x):
    return 1.0 / (1.0 + jnp.exp(-x))


def _silu(x):
    return x * _sigmoid(x)


def _in_proj_kernel(x_ref, nw_ref, w_ref, qkva_ref, z_ref, ba_ref, qb_ref, kb_ref, vb_ref):
    x = x_ref[...]
    h = x * lax.rsqrt(jnp.mean(x * x, axis=-1, keepdims=True) + NORM_EPS) * nw_ref[...]
    hb = h.astype(BF16)
    c0 = 3 * DN_WIDTH
    c1 = c0 + DN_WIDTH
    c2 = c1 + CA_WIDTH
    c3 = c2 + CA_WIDTH
    c4 = c3 + CA_WIDTH
    qkva_ref[...] = _dot(hb, w_ref[:, 0:c0])
    z_ref[...] = _dot(hb, w_ref[:, c0:c1])
    qb_ref[...] = _dot(hb, w_ref[:, c1:c2]).astype(BF16)
    kb_ref[...] = _dot(hb, w_ref[:, c2:c3]).astype(BF16)
    vb_ref[...] = _dot(hb, w_ref[:, c3:c4]).astype(BF16)
    ba_ref[...] = _dot(hb, w_ref[:, c4:c4 + LANES])


def _in_proj(x2, norm_w, w_all, tm):
    t, d = x2.shape
    wcols = w_all.shape[1]
    row = lambda i: (i, 0)
    fixed = lambda i: (0, 0)
    return pl.pallas_call(
        _in_proj_kernel,
        grid=(t // tm,),
        in_specs=[
            pl.BlockSpec((tm, d), row),
            pl.BlockSpec((1, d), fixed),
            pl.BlockSpec((d, wcols), fixed),
        ],
        out_specs=[
            pl.BlockSpec((tm, 3 * DN_WIDTH), row),
            pl.BlockSpec((tm, DN_WIDTH), row),
            pl.BlockSpec((tm, LANES), row),
            pl.BlockSpec((tm, CA_WIDTH), row),
            pl.BlockSpec((tm, CA_WIDTH), row),
            pl.BlockSpec((tm, CA_WIDTH), row),
        ],
        out_shape=[
            jax.ShapeDtypeStruct((t, 3 * DN_WIDTH), F32),
            jax.ShapeDtypeStruct((t, DN_WIDTH), F32),
            jax.ShapeDtypeStruct((t, LANES), F32),
            jax.ShapeDtypeStruct((t, CA_WIDTH), BF16),
            jax.ShapeDtypeStruct((t, CA_WIDTH), BF16),
            jax.ShapeDtypeStruct((t, CA_WIDTH), BF16),
        ],
        compiler_params=pltpu.CompilerParams(
            dimension_semantics=("parallel",), vmem_limit_bytes=VMEM_LIMIT),
        name="in_proj",
    )(x2, norm_w, w_all)


def _unit_lower_inverse(lower, eye, inv_block):
    diag = jnp.where(inv_block, lower, 0.0)
    off = lower - diag
    d2 = _bdot(diag, diag)
    d4 = _bdot(d2, d2)
    d8 = _bdot(d4, d4)
    p = _bdot(_bdot(eye - diag, eye + d2), _bdot(eye + d4, eye + d8))
    m = _bdot(p, off)
    m2 = _bdot(m, m)
    t0 = _bdot(_bdot(eye - m, eye + m2), p)
    resid = (eye - t0) - _dot(lower, t0, precision=HIGHEST)
    return t0 + _bdot(t0, resid)


def _dn_prep_kernel(cur_ref, halo_ref, ba_ref, convw_ref, alog_ref, dtb_ref,
                    w_ref, qd_ref, kd_ref, u_ref, qk_ref, cd_ref):
    n = pl.program_id(1)
    r = PREP_ROWS
    cur = cur_ref[0]
    halo = jnp.where(n > 0, halo_ref[0], 0.0)
    full = jnp.concatenate([halo, cur], axis=0)
    conv = full[SUBLANES:] * convw_ref[CONV_WIDTH - 1:CONV_WIDTH, :]
    for j in range(CONV_WIDTH - 1):
        shift = CONV_WIDTH - 1 - j
        conv = conv + pltpu.roll(full, shift, axis=0)[SUBLANES:] * convw_ref[j:j + 1, :]
    qkv = _silu(conv)

    ba = ba_ref[0]
    beta_all = _sigmoid(ba)
    sp_in = ba + dtb_ref[...]
    softplus = jnp.maximum(sp_in, 0.0) + jnp.log1p(jnp.exp(-jnp.abs(sp_in)))
    g_all = -jnp.exp(alog_ref[...]) * softplus
    rows = lax.broadcasted_iota(jnp.int32, (r, r), 0)
    cols = lax.broadcasted_iota(jnp.int32, (r, r), 1)
    same_chunk = (rows // CHUNK) == (cols // CHUNK)
    causal = same_chunk & (rows >= cols)
    strict = same_chunk & (rows > cols)
    inv_block = (rows // INV_BLOCK) == (cols // INV_BLOCK)
    eye = (rows == cols).astype(F32)
    gc_all = _dot(causal.astype(F32), g_all, precision=HIGHEST)
    gc_last_all = _dot(same_chunk.astype(F32), g_all, precision=HIGHEST)
    gc_rows = gc_all.T
    chunk_decay = jnp.exp(gc_last_all)
    cd_ref[0] = jnp.concatenate(
        [chunk_decay[c * CHUNK:c * CHUNK + SUBLANES] for c in range(r // CHUNK)], axis=0)

    scale = DN_HEAD_DIM ** -0.5
    qk_parts = []
    for h in range(DN_HEADS):
        lo = h * DN_HEAD_DIM
        q = qkv[:, lo:lo + DN_HEAD_DIM]
        k = qkv[:, DN_WIDTH + lo:DN_WIDTH + lo + DN_HEAD_DIM]
        v = qkv[:, 2 * DN_WIDTH + lo:2 * DN_WIDTH + lo + DN_HEAD_DIM]
        q = q * lax.rsqrt(jnp.sum(q * q, axis=-1, keepdims=True) + NORM_EPS) * scale
        k = k * lax.rsqrt(jnp.sum(k * k, axis=-1, keepdims=True) + NORM_EPS)
        beta = beta_all[:, h:h + 1]
        gc = gc_all[:, DN_HEADS + h:DN_HEADS + h + 1]
        gc_row = gc_rows[DN_HEADS + h:DN_HEADS + h + 1, :]
        gc_last = gc_last_all[:, DN_HEADS + h:DN_HEADS + h + 1]
        decay = jnp.where(causal, jnp.exp(jnp.where(causal, gc - gc_row, 0.0)), 0.0)
        egc = jnp.exp(gc)
        k_beta = k * beta
        k16 = k.astype(BF16)
        lower = jnp.where(strict, _dot_nt(k_beta.astype(BF16), k16) * decay, 0.0)
        t_inv = _unit_lower_inverse(lower, eye, inv_block)
        rhs = jnp.concatenate([v * beta, k_beta * egc], axis=-1)
        sol = _dot(t_inv, rhs, precision=HIGHEST)
        u_ref[0, :, lo:lo + DN_HEAD_DIM] = sol[:, :DN_HEAD_DIM]
        w_ref[0, :, lo:lo + DN_HEAD_DIM] = sol[:, DN_HEAD_DIM:].astype(BF16)
        qd_ref[0, :, lo:lo + DN_HEAD_DIM] = (q * egc).astype(BF16)
        kd_ref[0, :, lo:lo + DN_HEAD_DIM] = k * jnp.exp(gc_last - gc)
        qk = _dot_nt(q.astype(BF16), k16) * decay
        compact = qk[:, 0:CHUNK]
        for c in range(1, r // CHUNK):
            compact = compact + qk[:, c * CHUNK:(c + 1) * CHUNK]
        qk_parts.append(compact.astype(BF16))
    qk_ref[0] = jnp.concatenate(qk_parts, axis=-1)


def _dn_prep(qkva, ba, conv_w, alog_row, dtb_row):
    b, s, _ = qkva.shape
    r = PREP_ROWS
    n_chunks = s // CHUNK
    halo_blocks = r // SUBLANES
    fixed = lambda i, n: (0, 0)
    blk = lambda i, n: (i, n, 0)
    return pl.pallas_call(
        _dn_prep_kernel,
        grid=(b, s // r),
        in_specs=[
            pl.BlockSpec((1, r, 3 * DN_WIDTH), blk),
            pl.BlockSpec((1, SUBLANES, 3 * DN_WIDTH),
                         lambda i, n: (i, jnp.maximum(n * halo_blocks - 1, 0), 0)),
            pl.BlockSpec((1, r, LANES), blk),
            pl.BlockSpec((CONV_WIDTH, 3 * DN_WIDTH), fixed),
            pl.BlockSpec((1, LANES), fixed),
            pl.BlockSpec((1, LANES), fixed),
        ],
        out_specs=[
            pl.BlockSpec((1, r, DN_WIDTH), blk),
            pl.BlockSpec((1, r, DN_WIDTH), blk),
            pl.BlockSpec((1, r, DN_WIDTH), blk),
            pl.BlockSpec((1, r, DN_WIDTH), blk),
            pl.BlockSpec((1, r, DN_HEADS * CHUNK), blk),
            pl.BlockSpec((1, r // CHUNK * SUBLANES, LANES), blk),
        ],
        out_shape=[
            jax.ShapeDtypeStruct((b, s, DN_WIDTH), BF16),
            jax.ShapeDtypeStruct((b, s, DN_WIDTH), BF16),
            jax.ShapeDtypeStruct((b, s, DN_WIDTH), F32),
            jax.ShapeDtypeStruct((b, s, DN_WIDTH), F32),
            jax.ShapeDtypeStruct((b, s, DN_HEADS * CHUNK), BF16),
            jax.ShapeDtypeStruct((b, n_chunks * SUBLANES, LANES), F32),
        ],
        compiler_params=pltpu.CompilerParams(
            dimension_semantics=("parallel", "parallel"), vmem_limit_bytes=VMEM_LIMIT),
        name="dn_prep",
    )(qkva, qkva, ba, conv_w, alog_row, dtb_row)


def _dn_scan_kernel(w_ref, qd_ref, kd_ref, u_ref, qk_ref, cd_ref, z_ref, normw_ref, o_ref, state_ref):
    n = pl.program_id(0)

    @pl.when(n == 0)
    def _():
        state_ref[...] = jnp.zeros_like(state_ref)

    n_batch = w_ref.shape[0]
    for bi in range(n_batch):
        for h in range(DN_HEADS):
            lo = h * DN_HEAD_DIM
            state = state_ref[bi * DN_HEADS + h]
            s16 = state.astype(BF16)
            v_new = u_ref[bi, :, lo:lo + DN_HEAD_DIM] - _dot(w_ref[bi, :, lo:lo + DN_HEAD_DIM], s16)
            vn16 = v_new.astype(BF16)
            o = (_dot(qd_ref[bi, :, lo:lo + DN_HEAD_DIM], s16)
                 + _dot(qk_ref[bi, :, h * CHUNK:(h + 1) * CHUNK], vn16))
            kd_t = kd_ref[bi, :, lo:lo + DN_HEAD_DIM].T.astype(BF16)
            cd = cd_ref[bi, 0:1, DN_HEADS + h:DN_HEADS + h + 1]
            state_ref[bi * DN_HEADS + h] = state * cd + _dot(kd_t, vn16)
            o = o * lax.rsqrt(jnp.mean(o * o, axis=-1, keepdims=True) + NORM_EPS) * normw_ref[...]
            o_ref[bi, :, lo:lo + DN_HEAD_DIM] = o * _silu(z_ref[bi, :, lo:lo + DN_HEAD_DIM])


def _dn_scan(w, qd, kd, u, qk, cd, z, norm_w):
    b, s, _ = w.shape
    n_chunks = s // CHUNK
    blk = lambda n: (0, n, 0)
    wide = pl.BlockSpec((b, CHUNK, DN_WIDTH), blk)
    return pl.pallas_call(
        _dn_scan_kernel,
        grid=(n_chunks,),
        in_specs=[
            wide, wide, wide, wide,
            pl.BlockSpec((b, CHUNK, DN_HEADS * CHUNK), blk),
            pl.BlockSpec((b, SUBLANES, LANES), blk),
            wide,
            pl.BlockSpec((1, DN_HEAD_DIM), lambda n: (0, 0)),
        ],
        out_specs=wide,
        out_shape=jax.ShapeDtypeStruct((b, s, DN_WIDTH), F32),
        scratch_shapes=[pltpu.VMEM((b * DN_HEADS, DN_HEAD_DIM, DN_HEAD_DIM), F32)],
        compiler_params=pltpu.CompilerParams(
            dimension_semantics=("arbitrary",), vmem_limit_bytes=VMEM_LIMIT),
        name="dn_scan",
    )(w, qd, kd, u, qk, cd, z, norm_w)


def _band_attn_kernel(q_ref, k_ref, v_ref, bias_ref, nw_ref, o_ref):
    n = pl.program_id(1)
    band_chunks = CA_LEFT_CHUNKS + 1
    k_parts, v_parts = [], []
    for w in range(band_chunks):
        start = pl.multiple_of(jnp.maximum(n - CA_LEFT_CHUNKS + w, 0) * CHUNK, CHUNK)
        k_parts.append(k_ref[0, pl.ds(start, CHUNK), :])
        v_parts.append(v_ref[0, pl.ds(start, CHUNK), :])
    k_win = jnp.concatenate(k_parts, axis=0)
    v_win = jnp.concatenate(v_parts, axis=0)
    q = q_ref[0]

    key_pos = lax.broadcasted_iota(jnp.int32, (CHUNK, BAND), 1)
    valid = key_pos >= (CA_LEFT_CHUNKS - n) * CHUNK
    lane = lax.broadcasted_iota(jnp.int32, (CHUNK, LANES), 1)
    scale = CA_HEAD_DIM ** -0.5
    heads_per_group = LANES // CA_HEAD_DIM

    outs = []
    for grp in range(CA_WIDTH // LANES):
        lo = grp * LANES
        qg = q[:, lo:lo + LANES]
        kg = k_win[:, lo:lo + LANES]
        vg = v_win[:, lo:lo + LANES]
        og = jnp.zeros((CHUNK, LANES), F32)
        for hh in range(heads_per_group):
            head = grp * heads_per_group + hh
            in_head = (lane // CA_HEAD_DIM) == hh
            qm = jnp.where(in_head, qg, jnp.zeros_like(qg))
            s = _dot_nt(qm, kg) * scale + bias_ref[head]
            s = jnp.where(valid, s, -jnp.inf)
            e = jnp.exp(s - jnp.max(s, axis=-1, keepdims=True))
            p = e / jnp.sum(e, axis=-1, keepdims=True)
            og = jnp.where(in_head, _dot(p.astype(BF16), vg), og)
        outs.append(og)
    o = jnp.concatenate(outs, axis=-1)
    o = o * lax.rsqrt(jnp.mean(o * o, axis=-1, keepdims=True) + NORM_EPS) * nw_ref[...]
    o_ref[0] = o


def _band_attn(qb, kb, vb, bias, norm_w):
    b, s, _ = qb.shape
    n_chunks = s // CHUNK
    return pl.pallas_call(
        _band_attn_kernel,
        grid=(b, n_chunks),
        in_specs=[
            pl.BlockSpec((1, CHUNK, CA_WIDTH), lambda i, n: (i, n, 0)),
            pl.BlockSpec((1, s, CA_WIDTH), lambda i, n: (i, 0, 0)),
            pl.BlockSpec((1, s, CA_WIDTH), lambda i, n: (i, 0, 0)),
            pl.BlockSpec((CA_HEADS, CHUNK, BAND), lambda i, n: (0, 0, 0)),
            pl.BlockSpec((1, CA_WIDTH), lambda i, n: (0, 0)),
        ],
        out_specs=pl.BlockSpec((1, CHUNK, CA_WIDTH), lambda i, n: (i, n, 0)),
        out_shape=jax.ShapeDtypeStruct((b, s, CA_WIDTH), F32),
        compiler_params=pltpu.CompilerParams(
            dimension_semantics=("parallel", "arbitrary"), vmem_limit_bytes=VMEM_LIMIT),
        name="band_attn",
    )(qb, kb, vb, bias, norm_w)


def _out_router_kernel(oa_ref, ob_ref, x_ref, wout_ref, n2_ref, wr_ref, br_ref,
                       x1_ref, h2_ref, route_ref, gates_ref, counts_ref, run_ref):
    i = pl.program_id(0)

    @pl.when(i == 0)
    def _():
        run_ref[...] = jnp.zeros_like(run_ref)

    x1 = (x_ref[...]
          + _dot(oa_ref[...].astype(BF16), wout_ref[0:DN_WIDTH, :])
          + _dot(ob_ref[...].astype(BF16), wout_ref[DN_WIDTH:, :]))
    x1_ref[...] = x1
    h2 = x1 * lax.rsqrt(jnp.mean(x1 * x1, axis=-1, keepdims=True) + NORM_EPS) * n2_ref[...]
    h2_ref[...] = h2

    tm = x1.shape[0]
    lane = lax.broadcasted_iota(jnp.int32, (tm, LANES), 1)
    logits = _dot(h2, wr_ref[...], precision=HIGHEST) + br_ref[...]
    logits = jnp.where(lane < N_EXPERTS, logits, -jnp.inf)

    top_vals, top_idx, onehots = [], [], []
    for _ in range(TOP_K):
        m = jnp.max(logits, axis=-1, keepdims=True)
        idx = jnp.min(jnp.where(logits == m, lane, LANES), axis=-1, keepdims=True)
        hit = lane == idx
        top_vals.append(m)
        top_idx.append(idx)
        onehots.append(hit)
        logits = jnp.where(hit, -jnp.inf, logits)

    exps = [jnp.exp(v - top_vals[0]) for v in top_vals]
    denom = exps[0] + exps[1] + exps[2] + exps[3]
    gates = jnp.zeros((tm, LANES), F32)
    for k in range(TOP_K):
        gates = jnp.where(lane == k, exps[k] / denom, gates)
    gates_ref[...] = gates

    multi = jnp.zeros((tm, LANES), F32)
    for hit in onehots:
        multi = multi + hit.astype(F32)
    r = lax.broadcasted_iota(jnp.int32, (tm, tm), 0)
    c = lax.broadcasted_iota(jnp.int32, (tm, tm), 1)
    before = _dot((r > c).astype(BF16), multi.astype(BF16)) + run_ref[...]
    route = jnp.zeros((tm, LANES), jnp.int32)
    for k in range(TOP_K):
        rank = jnp.sum(jnp.where(onehots[k], before, 0.0), axis=-1, keepdims=True).astype(jnp.int32)
        route = jnp.where(lane == k, top_idx[k], route)
        route = jnp.where(lane == TOP_K + k, rank, route)
    route_ref[...] = route
    run_ref[...] = run_ref[...] + jnp.sum(multi, axis=0, keepdims=True)
    counts_ref[...] = run_ref[...]


def _out_router(oa, ob, x2, w_out, norm2_w, wr_pad, br_pad, tm):
    t, d = x2.shape
    row = lambda i: (i, 0)
    fixed = lambda i: (0, 0)
    return pl.pallas_call(
        _out_router_kernel,
        grid=(t // tm,),
        in_specs=[
            pl.BlockSpec((tm, DN_WIDTH), row),
            pl.BlockSpec((tm, CA_WIDTH), row),
            pl.BlockSpec((tm, d), row),
            pl.BlockSpec((DN_WIDTH + CA_WIDTH, d), fixed),
            pl.BlockSpec((1, d), fixed),
            pl.BlockSpec((d, LANES), fixed),
            pl.BlockSpec((1, LANES), fixed),
        ],
        out_specs=[
            pl.BlockSpec((tm, d), row),
            pl.BlockSpec((tm, d), row),
            pl.BlockSpec((tm, LANES), row),
            pl.BlockSpec((tm, LANES), row),
            pl.BlockSpec((1, LANES), fixed),
        ],
        out_shape=[
            jax.ShapeDtypeStruct((t, d), F32),
            jax.ShapeDtypeStruct((t, d), F32),
            jax.ShapeDtypeStruct((t, LANES), jnp.int32),
            jax.ShapeDtypeStruct((t, LANES), F32),
            jax.ShapeDtypeStruct((1, LANES), F32),
        ],
        scratch_shapes=[pltpu.VMEM((1, LANES), F32)],
        compiler_params=pltpu.CompilerParams(
            dimension_semantics=("arbitrary",), vmem_limit_bytes=VMEM_LIMIT),
        name="out_router",
    )(oa, ob, x2, w_out, norm2_w, wr_pad, br_pad)


def _experts_kernel(be_ref, src0_ref, src_next_ref, dst_prev_ref, dst_cur_ref, h2_ref,
                    wg_ref, bg_ref, wu_ref, bu_ref, wd_ref, bd_ref, g_ref,
                    xbuf, ybuf, wg16, wu16, wd16, sem_in, sem_out):
    b = pl.program_id(0)
    last = pl.num_programs(0) - 1
    bm = xbuf.shape[1]

    def gather_rows(idx_ref, slot):
        for r in range(bm):
            pltpu.make_async_copy(h2_ref.at[pl.ds(idx_ref[0, 0, r], 1)],
                                  xbuf.at[slot, pl.ds(r, 1)], sem_in.at[slot]).start()

    def scatter_rows(idx_ref, slot):
        for r in range(bm):
            pltpu.make_async_copy(ybuf.at[slot, pl.ds(r, 1)],
                                  g_ref.at[pl.ds(idx_ref[0, 0, r], 1)], sem_out.at[slot]).start()

    def wait_gather(slot):
        pltpu.make_async_copy(h2_ref.at[pl.ds(0, bm)], xbuf.at[slot], sem_in.at[slot]).wait()

    def wait_scatter(slot):
        pltpu.make_async_copy(ybuf.at[slot], g_ref.at[pl.ds(0, bm)], sem_out.at[slot]).wait()

    @pl.when(b == 0)
    def _():
        gather_rows(src0_ref, 0)
        ybuf[1] = jnp.zeros(ybuf.shape[1:], ybuf.dtype)

    changed = jnp.logical_or(b == 0, be_ref[b] != be_ref[jnp.maximum(b - 1, 0)])

    @pl.when(changed)
    def _():
        wg16[...] = wg_ref[0].astype(BF16)
        wu16[...] = wu_ref[0].astype(BF16)
        wd16[...] = wd_ref[0].astype(BF16)

    def step(cur):
        oth = 1 - cur
        wait_gather(cur)
        gather_rows(src_next_ref, oth)
        scatter_rows(dst_prev_ref, oth)
        x = xbuf[cur].astype(BF16)
        gate = _dot(x, wg16[...]) + bg_ref[0]
        up = _dot(x, wu16[...]) + bu_ref[0]
        gate = jnp.minimum(gate, SWIGLU_LIMIT)
        up = jnp.clip(up, -SWIGLU_LIMIT, SWIGLU_LIMIT)
        glu = gate * _sigmoid(gate * SWIGLU_ALPHA)
        mid = ((up + 1.0) * glu).astype(BF16)
        ybuf[cur] = _dot(mid, wd16[...]) + bd_ref[0]
        wait_scatter(oth)

        @pl.when(b == last)
        def _():
            scatter_rows(dst_cur_ref, cur)
            wait_scatter(cur)
            wait_gather(oth)

    @pl.when(b % 2 == 0)
    def _():
        step(0)

    @pl.when(b % 2 == 1)
    def _():
        step(1)


def _experts(h2, slot_src, slot_dst, block_e, n_out_rows, wg, bg, wu, bu, wd, bd, bm):
    d = h2.shape[1]
    f = wg.shape[2]
    n_blocks = slot_src.shape[0] // bm
    spare = n_out_rows - bm + jnp.arange(bm, dtype=jnp.int32)
    src3 = slot_src.reshape(n_blocks, 1, bm)
    dst3 = jnp.concatenate([spare, slot_dst]).reshape(n_blocks + 1, 1, bm)
    smem_blk = lambda imap: pl.BlockSpec((1, 1, bm), imap, memory_space=pltpu.SMEM)
    wmap = lambda b, be: (be[b], 0, 0)
    grid_spec = pltpu.PrefetchScalarGridSpec(
        num_scalar_prefetch=1,
        grid=(n_blocks,),
        in_specs=[
            smem_blk(lambda b, be: (0, 0, 0)),
            smem_blk(lambda b, be: (jnp.minimum(b + 1, n_blocks - 1), 0, 0)),
            smem_blk(lambda b, be: (b, 0, 0)),
            smem_blk(lambda b, be: (b + 1, 0, 0)),
            pl.BlockSpec(memory_space=pl.ANY),
            pl.BlockSpec((1, d, f), wmap),
            pl.BlockSpec((1, 1, f), wmap),
            pl.BlockSpec((1, d, f), wmap),
            pl.BlockSpec((1, 1, f), wmap),
            pl.BlockSpec((1, f, d), wmap),
            pl.BlockSpec((1, 1, d), wmap),
        ],
        out_specs=pl.BlockSpec(memory_space=pl.ANY),
        scratch_shapes=[
            pltpu.VMEM((2, bm, d), F32),
            pltpu.VMEM((2, bm, d), F32),
            pltpu.VMEM((d, f), BF16),
            pltpu.VMEM((d, f), BF16),
            pltpu.VMEM((f, d), BF16),
            pltpu.SemaphoreType.DMA((2,)),
            pltpu.SemaphoreType.DMA((2,)),
        ],
    )
    return pl.pallas_call(
        _experts_kernel,
        grid_spec=grid_spec,
        out_shape=jax.ShapeDtypeStruct((n_out_rows, d), F32),
        compiler_params=pltpu.CompilerParams(
            dimension_semantics=("arbitrary",), vmem_limit_bytes=VMEM_LIMIT),
        name="experts",
    )(block_e, src3, src3, dst3, dst3, h2, wg, bg, wu, bu, wd, bd)


def _combine_kernel(y0_ref, y1_ref, y2_ref, y3_ref, x1_ref, gates_ref, fw_ref, o_ref):
    acc = x1_ref[...]
    gates = gates_ref[...]
    for k, y_ref in enumerate((y0_ref, y1_ref, y2_ref, y3_ref)):
        acc = acc + y_ref[...] * gates[:, k:k + 1]
    o_ref[...] = acc * lax.rsqrt(jnp.mean(acc * acc, axis=-1, keepdims=True) + NORM_EPS) * fw_ref[...]


def _combine(ys, x1, gates, final_w, tm):
    t, d = x1.shape
    n_tiles = t // tm
    row = lambda i: (i, 0)
    y_spec = lambda k: pl.BlockSpec((tm, d), lambda i: (k * n_tiles + i, 0))
    return pl.pallas_call(
        _combine_kernel,
        grid=(n_tiles,),
        in_specs=[y_spec(k) for k in range(TOP_K)] + [
            pl.BlockSpec((tm, d), row),
            pl.BlockSpec((tm, LANES), row),
            pl.BlockSpec((1, d), lambda i: (0, 0)),
        ],
        out_specs=pl.BlockSpec((tm, d), row),
        out_shape=jax.ShapeDtypeStruct((t, d), F32),
        compiler_params=pltpu.CompilerParams(
            dimension_semantics=("parallel",), vmem_limit_bytes=VMEM_LIMIT),
        name="combine",
    )(ys, ys, ys, ys, x1, gates, final_w)


def _rel_bias_table(rel_bias):
    qi = jnp.arange(CHUNK)
    kj = jnp.arange(BAND)
    dist = qi[:, None] + CA_LEFT_CHUNKS * CHUNK - kj[None, :]
    idx = jnp.clip(dist, -MAX_REL_DIST, MAX_REL_DIST) + MAX_REL_DIST
    return rel_bias.astype(F32)[:, idx]


def _layer(x, norm1_w, w_in, conv_w, a_log, dt_bias, dn_norm_w, rel_bias, attn_norm_w, w_out,
           norm2_w, w_router, b_router, w_gate, b_gate, w_up, b_up, w_down, b_down, out_norm_w):
    b, s, d = x.shape
    t = b * s
    x2 = x.reshape(t, d)

    ba_lo = 4 * DN_WIDTH
    ba_hi = ba_lo + 2 * DN_HEADS
    w_small = jnp.pad(w_in[:, ba_lo:ba_hi], ((0, 0), (0, LANES - 2 * DN_HEADS)))
    w_all = jnp.concatenate([w_in[:, :ba_lo], w_in[:, ba_hi:], w_small], axis=1).astype(BF16)
    qkva, z_a, ba, q_b, k_b, v_b = _in_proj(x2, norm1_w.reshape(1, d), w_all, tm=512)

    lane_pad = (DN_HEADS, LANES - 2 * DN_HEADS)
    alog_row = jnp.pad(a_log.astype(F32), lane_pad).reshape(1, LANES)
    dtb_row = jnp.pad(dt_bias.astype(F32), lane_pad).reshape(1, LANES)
    dn_w, dn_qd, dn_kd, dn_u, dn_qk, dn_cd = _dn_prep(
        qkva.reshape(b, s, 3 * DN_WIDTH), ba.reshape(b, s, LANES), conv_w, alog_row, dtb_row)
    o_a = _dn_scan(dn_w, dn_qd, dn_kd, dn_u, dn_qk, dn_cd, z_a.reshape(b, s, DN_WIDTH),
                   dn_norm_w.reshape(1, DN_HEAD_DIM))
    o_b = _band_attn(q_b.reshape(b, s, CA_WIDTH), k_b.reshape(b, s, CA_WIDTH),
                     v_b.reshape(b, s, CA_WIDTH), _rel_bias_table(rel_bias),
                     attn_norm_w.reshape(1, CA_WIDTH))

    wr_pad = jnp.pad(w_router.astype(F32), ((0, 0), (0, LANES - N_EXPERTS)))
    br_pad = jnp.pad(b_router.astype(F32), (0, LANES - N_EXPERTS)).reshape(1, LANES)
    x1, h2, route, gates, counts = _out_router(
        o_a.reshape(t, DN_WIDTH), o_b.reshape(t, CA_WIDTH), x2, w_out.astype(BF16),
        norm2_w.reshape(1, d), wr_pad, br_pad, tm=512)

    bm = EXPERT_BLOCK_ROWS
    n_assign = t * TOP_K
    n_blocks = n_assign // bm + N_EXPERTS
    n_slots = n_blocks * bm
    counts_i = counts[0, :N_EXPERTS].astype(jnp.int32)
    padded = (counts_i + bm - 1) // bm * bm
    pad_end = jnp.cumsum(padded)
    pad_start = pad_end - padded
    top_e = route[:, :TOP_K]
    expert_ids = jnp.arange(N_EXPERTS, dtype=jnp.int32)
    start_of = jnp.sum(jnp.where(top_e[..., None] == expert_ids, pad_start, 0), axis=-1)
    pos = start_of + route[:, TOP_K:2 * TOP_K]
    assign = jnp.arange(n_assign, dtype=jnp.int32)
    slot_assign = jnp.full((n_slots,), -1, jnp.int32).at[pos.reshape(-1)].set(
        assign, unique_indices=True, mode='promise_in_bounds')
    n_out_rows = n_assign + bm
    slot_row = jnp.arange(n_slots, dtype=jnp.int32) % bm
    is_real = slot_assign >= 0
    slot_src = jnp.where(is_real, slot_assign // TOP_K, 0)
    slot_dst = jnp.where(is_real, (slot_assign % TOP_K) * t + slot_assign // TOP_K, n_assign + slot_row)
    block_start = jnp.arange(n_blocks, dtype=jnp.int32) * bm
    block_e = jnp.minimum(jnp.sum((pad_end[None, :] <= block_start[:, None]).astype(jnp.int32), axis=1),
                          N_EXPERTS - 1)

    f = w_gate.shape[-1]
    ys = _experts(h2, slot_src, slot_dst, block_e, n_out_rows,
                  w_gate, b_gate.reshape(N_EXPERTS, 1, f), w_up, b_up.reshape(N_EXPERTS, 1, f),
                  w_down, b_down.reshape(N_EXPERTS, 1, d), bm)
    out = _combine(ys, x1, gates, out_norm_w.reshape(1, d), tm=256)
    return out.reshape(b, s, d)


def kernel(x, norm1_w, w_in, conv_w, a_log, dt_bias, dn_norm_w, rel_bias, attn_norm_w, w_out, norm2_w, w_router, b_router, w_gate, b_gate, w_up, b_up, w_down, b_down, final_norm_w):
    depth = norm1_w.shape[0]
    assert depth == 1, "the final RMSNorm is fused into the last layer's combine step"
    return _layer(x, norm1_w[0], w_in[0], conv_w[0], a_log[0], dt_bias[0], dn_norm_w[0],
                  rel_bias[0], attn_norm_w[0], w_out[0], norm2_w[0], w_router[0], b_router[0],
                  w_gate[0], b_gate[0], w_up[0], b_up[0], w_down[0], b_down[0], final_norm_w)
```

```python
import jax
import numpy as np
import jax.numpy as jnp
from jax import lax
from jax.experimental import pallas as pl
from jax.experimental.pallas import tpu as pltpu

F32 = jnp.float32
BF16 = jnp.bfloat16
HIGHEST = lax.Precision.HIGHEST

NORM_EPS = 1e-6
CHUNK = 64
DN_HEADS = 4
DN_HEAD_DIM = 128
DN_WIDTH = DN_HEADS * DN_HEAD_DIM
CONV_WIDTH = 4
CA_HEADS = 8
CA_HEAD_DIM = 64
CA_WIDTH = CA_HEADS * CA_HEAD_DIM
CA_LEFT_CHUNKS = 8
MAX_REL_DIST = 256
N_EXPERTS = 32
TOP_K = 4
SWIGLU_LIMIT = 7.0
SWIGLU_ALPHA = 1.702

LANES = 128
SUBLANES = 8
INV_BLOCK = 16
PREP_ROWS = 256
ATT_ROWS = 256
ATT_WIN = ATT_ROWS + CA_LEFT_CHUNKS * CHUNK
ATT_EXT = 1024
EXPERT_BLOCK_ROWS = 256
VMEM_LIMIT = 48 * 1024 * 1024


def _dot(a, b, dims=(((1,), (0,)), ((), ())), precision=None):
    return lax.dot_general(a, b, dims, precision=precision, preferred_element_type=F32)


def _dot_nt(a, b, precision=None):
    return _dot(a, b, (((1,), (1,)), ((), ())), precision)


def _bdot(a, b):
    return _dot(a.astype(BF16), b.astype(BF16))


def _dot_split(a, b, b_hi):
    a_hi = a.astype(BF16)
    a_lo = (a - a_hi.astype(F32)).astype(BF16)
    b_lo = (b - b_hi.astype(F32)).astype(BF16)
    return _dot(a_hi, b_hi) + (_dot(a_hi, b_lo) + _dot(a_lo, b_hi))


def _store_row_slabs(ref, value, rows):
    for j in range(SUBLANES):
        ref[pl.ds(j, rows, stride=SUBLANES), :] = value[:, j * LANES:(j + 1) * LANES]


def _load_row_slabs(ref, rows):
    return [ref[pl.ds(j, rows, stride=SUBLANES), :] for j in range(SUBLANES)]


def _sigmoid(x):
    return 1.0 / (1.0 + jnp.exp(-x))


def _silu(x):
    return x * _sigmoid(x)


def _in_proj_kernel(x_ref, nw_ref, w_ref, qkva_ref, z_ref, ba_ref, qb_ref, kb_ref, vb_ref):
    x = x_ref[...]
    h = x * lax.rsqrt(jnp.mean(x * x, axis=-1, keepdims=True) + NORM_EPS) * nw_ref[...]
    hb = h.astype(BF16)
    c0 = 3 * DN_WIDTH
    c1 = c0 + DN_WIDTH
    c2 = c1 + CA_WIDTH
    c3 = c2 + CA_WIDTH
    c4 = c3 + CA_WIDTH
    qkva_ref[...] = _dot(hb, w_ref[:, 0:c0])
    z_ref[...] = _dot(hb, w_ref[:, c0:c1])
    qb_ref[...] = _dot(hb, w_ref[:, c1:c2]).astype(BF16)
    kb_ref[...] = _dot(hb, w_ref[:, c2:c3]).astype(BF16)
    vb_ref[...] = _dot(hb, w_ref[:, c3:c4]).astype(BF16)
    ba_ref[...] = _dot(hb, w_ref[:, c4:c4 + LANES])


def _in_proj(x2, norm_w, w_all, tm):
    t, d = x2.shape
    wcols = w_all.shape[1]
    row = lambda i: (i, 0)
    fixed = lambda i: (0, 0)
    return pl.pallas_call(
        _in_proj_kernel,
        grid=(t // tm,),
        in_specs=[
            pl.BlockSpec((tm, d), row),
            pl.BlockSpec((1, d), fixed),
            pl.BlockSpec((d, wcols), fixed),
        ],
        out_specs=[
            pl.BlockSpec((tm, 3 * DN_WIDTH), row),
            pl.BlockSpec((tm, DN_WIDTH), row),
            pl.BlockSpec((tm, LANES), row),
            pl.BlockSpec((tm, CA_WIDTH), row),
            pl.BlockSpec((tm, CA_WIDTH), row),
            pl.BlockSpec((tm, CA_WIDTH), row),
        ],
        out_shape=[
            jax.ShapeDtypeStruct((t, 3 * DN_WIDTH), F32),
            jax.ShapeDtypeStruct((t, DN_WIDTH), F32),
            jax.ShapeDtypeStruct((t, LANES), F32),
            jax.ShapeDtypeStruct((t, CA_WIDTH), BF16),
            jax.ShapeDtypeStruct((t, CA_WIDTH), BF16),
            jax.ShapeDtypeStruct((t, CA_WIDTH), BF16),
        ],
        compiler_params=pltpu.CompilerParams(
            dimension_semantics=("parallel",), vmem_limit_bytes=VMEM_LIMIT),
        name="in_proj",
    )(x2, norm_w, w_all)


def _unit_lower_inverse(lower, eye, inv_block):
    diag = jnp.where(inv_block, lower, 0.0)
    off = lower - diag
    d2 = _bdot(diag, diag)
    d4 = _bdot(d2, d2)
    d8 = _bdot(d4, d4)
    p = _bdot(_bdot(eye - diag, eye + d2), _bdot(eye + d4, eye + d8))
    m = _bdot(p, off)
    m2 = _bdot(m, m)
    t0 = _bdot(_bdot(eye - m, eye + m2), p)
    t0_hi = t0.astype(BF16)
    resid = (eye - t0) - _dot_split(lower, t0, t0_hi)
    return t0 + _dot(t0_hi, resid.astype(BF16))


def _dn_prep_kernel(cur_ref, halo_ref, ba_ref, convw_ref, alog_ref, dtb_ref,
                    w_ref, qd_ref, kd_ref, u_ref, qk_ref, cd_ref):
    n = pl.program_id(1)
    r = PREP_ROWS
    cur = cur_ref[0]
    halo = jnp.where(n > 0, halo_ref[0], 0.0)
    full = jnp.concatenate([halo, cur], axis=0)
    conv = full[SUBLANES:] * convw_ref[CONV_WIDTH - 1:CONV_WIDTH, :]
    for j in range(CONV_WIDTH - 1):
        shift = CONV_WIDTH - 1 - j
        conv = conv + pltpu.roll(full, shift, axis=0)[SUBLANES:] * convw_ref[j:j + 1, :]
    qkv = _silu(conv)

    ba = ba_ref[0]
    beta_all = _sigmoid(ba)
    sp_in = ba + dtb_ref[...]
    softplus = jnp.maximum(sp_in, 0.0) + jnp.log1p(jnp.exp(-jnp.abs(sp_in)))
    g_all = -jnp.exp(alog_ref[...]) * softplus
    rows = lax.broadcasted_iota(jnp.int32, (r, r), 0)
    cols = lax.broadcasted_iota(jnp.int32, (r, r), 1)
    same_chunk = (rows // CHUNK) == (cols // CHUNK)
    causal = same_chunk & (rows >= cols)
    strict = same_chunk & (rows > cols)
    inv_block = (rows // INV_BLOCK) == (cols // INV_BLOCK)
    eye = (rows == cols).astype(F32)
    gc_all = _dot(causal.astype(F32), g_all, precision=HIGHEST)
    gc_last_all = _dot(same_chunk.astype(F32), g_all, precision=HIGHEST)
    gc_rows = gc_all.T
    chunk_decay = jnp.exp(gc_last_all)
    cd_ref[0] = jnp.concatenate(
        [chunk_decay[c * CHUNK:c * CHUNK + SUBLANES] for c in range(r // CHUNK)], axis=0)

    scale = DN_HEAD_DIM ** -0.5
    qk_parts = []
    for h in range(DN_HEADS):
        lo = h * DN_HEAD_DIM
        q = qkv[:, lo:lo + DN_HEAD_DIM]
        k = qkv[:, DN_WIDTH + lo:DN_WIDTH + lo + DN_HEAD_DIM]
        v = qkv[:, 2 * DN_WIDTH + lo:2 * DN_WIDTH + lo + DN_HEAD_DIM]
        q = q * lax.rsqrt(jnp.sum(q * q, axis=-1, keepdims=True) + NORM_EPS) * scale
        k = k * lax.rsqrt(jnp.sum(k * k, axis=-1, keepdims=True) + NORM_EPS)
        beta = beta_all[:, h:h + 1]
        gc = gc_all[:, DN_HEADS + h:DN_HEADS + h + 1]
        gc_row = gc_rows[DN_HEADS + h:DN_HEADS + h + 1, :]
        gc_last = gc_last_all[:, DN_HEADS + h:DN_HEADS + h + 1]
        decay = jnp.where(causal, jnp.exp(jnp.where(causal, gc - gc_row, 0.0)), 0.0)
        egc = jnp.exp(gc)
        k_beta = k * beta
        k16 = k.astype(BF16)
        lower = jnp.where(strict, _dot_nt(k_beta.astype(BF16), k16) * decay, 0.0)
        t_inv = _unit_lower_inverse(lower, eye, inv_block)
        rhs = jnp.concatenate([v * beta, k_beta * egc], axis=-1)
        sol = _bdot(t_inv, rhs)
        u_ref[0, :, lo:lo + DN_HEAD_DIM] = sol[:, :DN_HEAD_DIM]
        w_ref[0, :, lo:lo + DN_HEAD_DIM] = sol[:, DN_HEAD_DIM:].astype(BF16)
        qd_ref[0, :, lo:lo + DN_HEAD_DIM] = (q * egc).astype(BF16)
        kd_ref[0, :, lo:lo + DN_HEAD_DIM] = k * jnp.exp(gc_last - gc)
        qk = _dot_nt(q.astype(BF16), k16) * decay
        compact = qk[:, 0:CHUNK]
        for c in range(1, r // CHUNK):
            compact = compact + qk[:, c * CHUNK:(c + 1) * CHUNK]
        qk_parts.append(compact.astype(BF16))
    qk_ref[0] = jnp.concatenate(qk_parts, axis=-1)


def _dn_prep(qkva, ba, conv_w, alog_row, dtb_row):
    b, s, _ = qkva.shape
    r = PREP_ROWS
    n_chunks = s // CHUNK
    halo_blocks = r // SUBLANES
    fixed = lambda i, n: (0, 0)
    blk = lambda i, n: (i, n, 0)
    return pl.pallas_call(
        _dn_prep_kernel,
        grid=(b, s // r),
        in_specs=[
            pl.BlockSpec((1, r, 3 * DN_WIDTH), blk),
            pl.BlockSpec((1, SUBLANES, 3 * DN_WIDTH),
                         lambda i, n: (i, jnp.maximum(n * halo_blocks - 1, 0), 0)),
            pl.BlockSpec((1, r, LANES), blk),
            pl.BlockSpec((CONV_WIDTH, 3 * DN_WIDTH), fixed),
            pl.BlockSpec((1, LANES), fixed),
            pl.BlockSpec((1, LANES), fixed),
        ],
        out_specs=[
            pl.BlockSpec((1, r, DN_WIDTH), blk),
            pl.BlockSpec((1, r, DN_WIDTH), blk),
            pl.BlockSpec((1, r, DN_WIDTH), blk),
            pl.BlockSpec((1, r, DN_WIDTH), blk),
            pl.BlockSpec((1, r, DN_HEADS * CHUNK), blk),
            pl.BlockSpec((1, r // CHUNK * SUBLANES, LANES), blk),
        ],
        out_shape=[
            jax.ShapeDtypeStruct((b, s, DN_WIDTH), BF16),
            jax.ShapeDtypeStruct((b, s, DN_WIDTH), BF16),
            jax.ShapeDtypeStruct((b, s, DN_WIDTH), F32),
            jax.ShapeDtypeStruct((b, s, DN_WIDTH), F32),
            jax.ShapeDtypeStruct((b, s, DN_HEADS * CHUNK), BF16),
            jax.ShapeDtypeStruct((b, n_chunks * SUBLANES, LANES), F32),
        ],
        compiler_params=pltpu.CompilerParams(
            dimension_semantics=("parallel", "parallel"), vmem_limit_bytes=VMEM_LIMIT),
        name="dn_prep",
    )(qkva, qkva, ba, conv_w, alog_row, dtb_row)


def _dn_scan_kernel(w_ref, qd_ref, kd_ref, u_ref, qk_ref, cd_ref, z_ref, normw_ref, o_ref, state_ref):
    n = pl.program_id(0)

    @pl.when(n == 0)
    def _():
        state_ref[...] = jnp.zeros_like(state_ref)

    n_batch = w_ref.shape[0]
    for bi in range(n_batch):
        for h in range(DN_HEADS):
            lo = h * DN_HEAD_DIM
            state = state_ref[bi * DN_HEADS + h]
            s16 = state.astype(BF16)
            v_new = u_ref[bi, :, lo:lo + DN_HEAD_DIM] - _dot(w_ref[bi, :, lo:lo + DN_HEAD_DIM], s16)
            vn16 = v_new.astype(BF16)
            o = (_dot(qd_ref[bi, :, lo:lo + DN_HEAD_DIM], s16)
                 + _dot(qk_ref[bi, :, h * CHUNK:(h + 1) * CHUNK], vn16))
            kd_t = kd_ref[bi, :, lo:lo + DN_HEAD_DIM].T.astype(BF16)
            cd = cd_ref[bi, 0:1, DN_HEADS + h:DN_HEADS + h + 1]
            state_ref[bi * DN_HEADS + h] = state * cd + _dot(kd_t, vn16)
            o = o * lax.rsqrt(jnp.mean(o * o, axis=-1, keepdims=True) + NORM_EPS) * normw_ref[...]
            o_ref[bi, :, lo:lo + DN_HEAD_DIM] = o * _silu(z_ref[bi, :, lo:lo + DN_HEAD_DIM])


def _dn_scan(w, qd, kd, u, qk, cd, z, norm_w):
    b, s, _ = w.shape
    n_chunks = s // CHUNK
    blk = lambda n: (0, n, 0)
    wide = pl.BlockSpec((b, CHUNK, DN_WIDTH), blk)
    return pl.pallas_call(
        _dn_scan_kernel,
        grid=(n_chunks,),
        in_specs=[
            wide, wide, wide, wide,
            pl.BlockSpec((b, CHUNK, DN_HEADS * CHUNK), blk),
            pl.BlockSpec((b, SUBLANES, LANES), blk),
            wide,
            pl.BlockSpec((1, DN_HEAD_DIM), lambda n: (0, 0)),
        ],
        out_specs=wide,
        out_shape=jax.ShapeDtypeStruct((b, s, DN_WIDTH), F32),
        scratch_shapes=[pltpu.VMEM((b * DN_HEADS, DN_HEAD_DIM, DN_HEAD_DIM), F32)],
        compiler_params=pltpu.CompilerParams(
            dimension_semantics=("arbitrary",), vmem_limit_bytes=VMEM_LIMIT),
        name="dn_scan",
    )(w, qd, kd, u, qk, cd, z, norm_w)


def _band_attn_kernel(q_ref, k_ref, v_ref, ext_ref, nw_ref, o_ref, bias_ref):
    first = jnp.logical_and(pl.program_id(0) == 0, pl.program_id(1) == 0)

    @pl.when(first)
    def _():
        qi = lax.broadcasted_iota(jnp.int32, (ATT_ROWS, ATT_WIN), 0) // CHUNK
        kj = lax.broadcasted_iota(jnp.int32, (ATT_ROWS, ATT_WIN), 1) // CHUNK
        in_band = jnp.logical_and(kj >= qi, kj <= qi + CA_LEFT_CHUNKS)
        for h in range(CA_HEADS):
            tiled = jnp.broadcast_to(ext_ref[h:h + 1, :], (ATT_ROWS, ATT_EXT))
            toeplitz = pltpu.roll(tiled, 0, axis=1, stride=1, stride_axis=0)
            bias_ref[h] = jnp.where(in_band, toeplitz[:, :ATT_WIN], -jnp.inf)

    n0 = pl.program_id(1) * (ATT_ROWS // CHUNK)
    k_parts, v_parts = [], []
    for w in range(ATT_WIN // CHUNK):
        start = pl.multiple_of(jnp.maximum(n0 - CA_LEFT_CHUNKS + w, 0) * CHUNK, CHUNK)
        k_parts.append(k_ref[0, pl.ds(start, CHUNK), :])
        v_parts.append(v_ref[0, pl.ds(start, CHUNK), :])
    k_win = jnp.concatenate(k_parts, axis=0)
    v_win = jnp.concatenate(v_parts, axis=0)
    q = q_ref[0] * jnp.asarray(CA_HEAD_DIM ** -0.5, BF16)

    key_col = lax.broadcasted_iota(jnp.int32, (1, ATT_WIN), 1)
    before_start = key_col < (CA_LEFT_CHUNKS - n0) * CHUNK
    col_bias = jnp.where(before_start, -jnp.inf, 0.0)
    lane = lax.broadcasted_iota(jnp.int32, (ATT_ROWS, LANES), 1)
    heads_per_group = LANES // CA_HEAD_DIM

    outs = []
    for grp in range(CA_WIDTH // LANES):
        lo = grp * LANES
        qg = q[:, lo:lo + LANES]
        kg = k_win[:, lo:lo + LANES]
        vg = v_win[:, lo:lo + LANES]
        og = jnp.zeros((ATT_ROWS, LANES), F32)
        for hh in range(heads_per_group):
            head = grp * heads_per_group + hh
            in_head = (lane // CA_HEAD_DIM) == hh
            qm = jnp.where(in_head, qg, jnp.zeros_like(qg))
            s = _dot_nt(qm, kg) + bias_ref[head] + col_bias
            e = jnp.exp(s - jnp.max(s, axis=-1, keepdims=True))
            denom = jnp.sum(e, axis=-1, keepdims=True)
            og = jnp.where(in_head, _dot(e.astype(BF16), vg) / denom, og)
        outs.append(og)
    o = jnp.concatenate(outs, axis=-1)
    o = o * lax.rsqrt(jnp.mean(o * o, axis=-1, keepdims=True) + NORM_EPS) * nw_ref[...]
    o_ref[0] = o


def _band_attn(qb, kb, vb, ext, norm_w):
    b, s, _ = qb.shape
    return pl.pallas_call(
        _band_attn_kernel,
        grid=(b, s // ATT_ROWS),
        in_specs=[
            pl.BlockSpec((1, ATT_ROWS, CA_WIDTH), lambda i, n: (i, n, 0)),
            pl.BlockSpec((1, s, CA_WIDTH), lambda i, n: (i, 0, 0)),
            pl.BlockSpec((1, s, CA_WIDTH), lambda i, n: (i, 0, 0)),
            pl.BlockSpec((CA_HEADS, ATT_EXT), lambda i, n: (0, 0)),
            pl.BlockSpec((1, CA_WIDTH), lambda i, n: (0, 0)),
        ],
        out_specs=pl.BlockSpec((1, ATT_ROWS, CA_WIDTH), lambda i, n: (i, n, 0)),
        out_shape=jax.ShapeDtypeStruct((b, s, CA_WIDTH), F32),
        scratch_shapes=[pltpu.VMEM((CA_HEADS, ATT_ROWS, ATT_WIN), F32)],
        compiler_params=pltpu.CompilerParams(
            dimension_semantics=("arbitrary", "arbitrary"), vmem_limit_bytes=VMEM_LIMIT),
        name="band_attn",
    )(qb, kb, vb, ext, norm_w)


def _out_router_kernel(oa_ref, ob_ref, x_ref, wout_ref, n2_ref, wr_ref, br_ref,
                       x1_ref, h2_ref, route_ref, gates_ref, counts_ref, run_ref):
    i = pl.program_id(0)

    @pl.when(i == 0)
    def _():
        run_ref[...] = jnp.zeros_like(run_ref)

    x1 = (x_ref[...]
          + _dot(oa_ref[...].astype(BF16), wout_ref[0:DN_WIDTH, :])
          + _dot(ob_ref[...].astype(BF16), wout_ref[DN_WIDTH:, :]))
    x1_ref[...] = x1
    h2 = x1 * lax.rsqrt(jnp.mean(x1 * x1, axis=-1, keepdims=True) + NORM_EPS) * n2_ref[...]
    tm = x1.shape[0]
    _store_row_slabs(h2_ref, h2, tm)
    lane = lax.broadcasted_iota(jnp.int32, (tm, LANES), 1)
    logits = _dot(h2, wr_ref[...], precision=HIGHEST) + br_ref[...]
    logits = jnp.where(lane < N_EXPERTS, logits, -jnp.inf)

    top_vals, top_idx, onehots = [], [], []
    for _ in range(TOP_K):
        m = jnp.max(logits, axis=-1, keepdims=True)
        idx = jnp.min(jnp.where(logits == m, lane, LANES), axis=-1, keepdims=True)
        hit = lane == idx
        top_vals.append(m)
        top_idx.append(idx)
        onehots.append(hit)
        logits = jnp.where(hit, -jnp.inf, logits)

    exps = [jnp.exp(v - top_vals[0]) for v in top_vals]
    denom = exps[0] + exps[1] + exps[2] + exps[3]
    gates = jnp.zeros((tm, LANES), F32)
    for k in range(TOP_K):
        gates = jnp.where(lane == k, exps[k] / denom, gates)
    gates_ref[...] = gates

    multi = jnp.zeros((tm, LANES), F32)
    for hit in onehots:
        multi = multi + hit.astype(F32)
    r = lax.broadcasted_iota(jnp.int32, (tm, tm), 0)
    c = lax.broadcasted_iota(jnp.int32, (tm, tm), 1)
    before = _dot((r > c).astype(BF16), multi.astype(BF16)) + run_ref[...]
    route = jnp.zeros((tm, LANES), jnp.int32)
    for k in range(TOP_K):
        rank = jnp.sum(jnp.where(onehots[k], before, 0.0), axis=-1, keepdims=True).astype(jnp.int32)
        route = jnp.where(lane == k, top_idx[k], route)
        route = jnp.where(lane == TOP_K + k, rank, route)
    route_ref[...] = route
    run_ref[...] = run_ref[...] + jnp.sum(multi, axis=0, keepdims=True)
    counts_ref[...] = run_ref[...]


def _out_router(oa, ob, x2, w_out, norm2_w, wr_pad, br_pad, tm):
    t, d = x2.shape
    row = lambda i: (i, 0)
    fixed = lambda i: (0, 0)
    return pl.pallas_call(
        _out_router_kernel,
        grid=(t // tm,),
        in_specs=[
            pl.BlockSpec((tm, DN_WIDTH), row),
            pl.BlockSpec((tm, CA_WIDTH), row),
            pl.BlockSpec((tm, d), row),
            pl.BlockSpec((DN_WIDTH + CA_WIDTH, d), fixed),
            pl.BlockSpec((1, d), fixed),
            pl.BlockSpec((d, LANES), fixed),
            pl.BlockSpec((1, LANES), fixed),
        ],
        out_specs=[
            pl.BlockSpec((tm, d), row),
            pl.BlockSpec((tm * SUBLANES, LANES), row),
            pl.BlockSpec((tm, LANES), row),
            pl.BlockSpec((tm, LANES), row),
            pl.BlockSpec((1, LANES), fixed),
        ],
        out_shape=[
            jax.ShapeDtypeStruct((t, d), F32),
            jax.ShapeDtypeStruct((t * SUBLANES, LANES), F32),
            jax.ShapeDtypeStruct((t, LANES), jnp.int32),
            jax.ShapeDtypeStruct((t, LANES), F32),
            jax.ShapeDtypeStruct((1, LANES), F32),
        ],
        scratch_shapes=[pltpu.VMEM((1, LANES), F32)],
        compiler_params=pltpu.CompilerParams(
            dimension_semantics=("arbitrary",), vmem_limit_bytes=VMEM_LIMIT),
        name="out_router",
    )(oa, ob, x2, w_out, norm2_w, wr_pad, br_pad)


def _experts_kernel(be_ref, src0_ref, src_next_ref, dst_prev_ref, dst_cur_ref, h2_ref,
                    wg_ref, bg_ref, wu_ref, bu_ref, wd_ref, bd_ref, g_ref,
                    xa, xb, ya, yb, wg16, wu16, wd16, sem_in, sem_out):
    b = pl.program_id(0)
    last = pl.num_programs(0) - 1
    slab = SUBLANES
    bm = xa.shape[0] // slab
    xbufs = (xa, xb)
    ybufs = (ya, yb)

    def gather_rows(idx_ref, slot):
        for r in range(bm):
            src = pl.multiple_of(idx_ref[0, 0, r], slab)
            pltpu.make_async_copy(h2_ref.at[pl.ds(src, slab)],
                                  xbufs[slot].at[pl.ds(r * slab, slab)], sem_in.at[slot]).start()

    def scatter_rows(idx_ref, slot):
        for r in range(bm):
            dst = pl.multiple_of(idx_ref[0, 0, r], slab)
            pltpu.make_async_copy(ybufs[slot].at[pl.ds(r * slab, slab)],
                                  g_ref.at[pl.ds(dst, slab)], sem_out.at[slot]).start()

    def wait_gather(slot):
        pltpu.make_async_copy(h2_ref.at[pl.ds(0, bm * slab)], xbufs[slot], sem_in.at[slot]).wait()

    def wait_scatter(slot):
        pltpu.make_async_copy(ybufs[slot], g_ref.at[pl.ds(0, bm * slab)], sem_out.at[slot]).wait()

    @pl.when(b == 0)
    def _():
        gather_rows(src0_ref, 0)
        yb[...] = jnp.zeros(yb.shape, yb.dtype)

    changed = jnp.logical_or(b == 0, be_ref[b] != be_ref[jnp.maximum(b - 1, 0)])

    @pl.when(changed)
    def _():
        wg16[...] = wg_ref[0].astype(BF16)
        wu16[...] = wu_ref[0].astype(BF16)
        wd16[...] = wd_ref[0].astype(BF16)

    def step(cur):
        oth = 1 - cur
        wait_gather(cur)
        gather_rows(src_next_ref, oth)
        scatter_rows(dst_prev_ref, oth)
        x = jnp.concatenate(_load_row_slabs(xbufs[cur], bm), axis=-1).astype(BF16)
        gate = _dot(x, wg16[...]) + bg_ref[0]
        up = _dot(x, wu16[...]) + bu_ref[0]
        gate = jnp.minimum(gate, SWIGLU_LIMIT)
        up = jnp.clip(up, -SWIGLU_LIMIT, SWIGLU_LIMIT)
        glu = gate * _sigmoid(gate * SWIGLU_ALPHA)
        mid = ((up + 1.0) * glu).astype(BF16)
        _store_row_slabs(ybufs[cur], _dot(mid, wd16[...]) + bd_ref[0], bm)
        wait_scatter(oth)

        @pl.when(b == last)
        def _():
            scatter_rows(dst_cur_ref, cur)
            wait_scatter(cur)
            wait_gather(oth)

    @pl.when(b % 2 == 0)
    def _():
        step(0)

    @pl.when(b % 2 == 1)
    def _():
        step(1)


def _experts(h2, slot_src, slot_dst, block_e, n_out_rows, wg, bg, wu, bu, wd, bd, bm):
    slab = SUBLANES
    d = wg.shape[1]
    f = wg.shape[2]
    n_blocks = slot_src.shape[0] // bm
    spare = n_out_rows - bm + jnp.arange(bm, dtype=jnp.int32)
    src3 = (slot_src * slab).reshape(n_blocks, 1, bm)
    dst3 = (jnp.concatenate([spare, slot_dst]) * slab).reshape(n_blocks + 1, 1, bm)
    smem_blk = lambda imap: pl.BlockSpec((1, 1, bm), imap, memory_space=pltpu.SMEM)
    wmap = lambda b, be: (be[b], 0, 0)
    grid_spec = pltpu.PrefetchScalarGridSpec(
        num_scalar_prefetch=1,
        grid=(n_blocks,),
        in_specs=[
            smem_blk(lambda b, be: (0, 0, 0)),
            smem_blk(lambda b, be: (jnp.minimum(b + 1, n_blocks - 1), 0, 0)),
            smem_blk(lambda b, be: (b, 0, 0)),
            smem_blk(lambda b, be: (b + 1, 0, 0)),
            pl.BlockSpec(memory_space=pl.ANY),
            pl.BlockSpec((1, d, f), wmap),
            pl.BlockSpec((1, 1, f), wmap),
            pl.BlockSpec((1, d, f), wmap),
            pl.BlockSpec((1, 1, f), wmap),
            pl.BlockSpec((1, f, d), wmap),
            pl.BlockSpec((1, 1, d), wmap),
        ],
        out_specs=pl.BlockSpec(memory_space=pl.ANY),
        scratch_shapes=[
            pltpu.VMEM((bm * slab, LANES), F32),
            pltpu.VMEM((bm * slab, LANES), F32),
            pltpu.VMEM((bm * slab, LANES), F32),
            pltpu.VMEM((bm * slab, LANES), F32),
            pltpu.VMEM((d, f), BF16),
            pltpu.VMEM((d, f), BF16),
            pltpu.VMEM((f, d), BF16),
            pltpu.SemaphoreType.DMA((2,)),
            pltpu.SemaphoreType.DMA((2,)),
        ],
    )
    return pl.pallas_call(
        _experts_kernel,
        grid_spec=grid_spec,
        out_shape=jax.ShapeDtypeStruct((n_out_rows * slab, LANES), F32),
        compiler_params=pltpu.CompilerParams(
            dimension_semantics=("arbitrary",), vmem_limit_bytes=VMEM_LIMIT),
        name="experts",
    )(block_e, src3, src3, dst3, dst3, h2, wg, bg, wu, bu, wd, bd)


def _combine_kernel(y0_ref, y1_ref, y2_ref, y3_ref, x1_ref, gates_ref, fw_ref, o_ref):
    tm = x1_ref.shape[0]
    acc = x1_ref[...]
    gates = gates_ref[...]
    for k, y_ref in enumerate((y0_ref, y1_ref, y2_ref, y3_ref)):
        y = jnp.concatenate(_load_row_slabs(y_ref, tm), axis=-1)
        acc = acc + y * gates[:, k:k + 1]
    o_ref[...] = acc * lax.rsqrt(jnp.mean(acc * acc, axis=-1, keepdims=True) + NORM_EPS) * fw_ref[...]


def _combine(ys, x1, gates, final_w, tm):
    t, d = x1.shape
    n_tiles = t // tm
    row = lambda i: (i, 0)
    y_spec = lambda k: pl.BlockSpec((tm * SUBLANES, LANES), lambda i: (k * n_tiles + i, 0))
    return pl.pallas_call(
        _combine_kernel,
        grid=(n_tiles,),
        in_specs=[y_spec(k) for k in range(TOP_K)] + [
            pl.BlockSpec((tm, d), row),
            pl.BlockSpec((tm, LANES), row),
            pl.BlockSpec((1, d), lambda i: (0, 0)),
        ],
        out_specs=pl.BlockSpec((tm, d), row),
        out_shape=jax.ShapeDtypeStruct((t, d), F32),
        compiler_params=pltpu.CompilerParams(
            dimension_semantics=("parallel",), vmem_limit_bytes=VMEM_LIMIT),
        name="combine",
    )(ys, ys, ys, ys, x1, gates, final_w)


def _rel_bias_rows(rel_bias):
    m = np.arange(ATT_EXT)
    m = np.where(m < ATT_WIN, m, m - ATT_EXT)
    dist = CA_LEFT_CHUNKS * CHUNK - m
    idx = np.clip(dist, -MAX_REL_DIST, MAX_REL_DIST) + MAX_REL_DIST
    return rel_bias.astype(F32)[:, idx]


def _layer(x, norm1_w, w_in, conv_w, a_log, dt_bias, dn_norm_w, rel_bias, attn_norm_w, w_out,
           norm2_w, w_router, b_router, w_gate, b_gate, w_up, b_up, w_down, b_down, out_norm_w):
    b, s, d = x.shape
    assert d == SUBLANES * LANES, "row-slab layout holds one model row per (8, 128) tile"
    assert s % ATT_ROWS == 0 and s % PREP_ROWS == 0
    t = b * s
    x2 = x.reshape(t, d)

    ba_lo = 4 * DN_WIDTH
    ba_hi = ba_lo + 2 * DN_HEADS
    w_small = jnp.pad(w_in[:, ba_lo:ba_hi], ((0, 0), (0, LANES - 2 * DN_HEADS)))
    w_all = jnp.concatenate([w_in[:, :ba_lo], w_in[:, ba_hi:], w_small], axis=1).astype(BF16)
    qkva, z_a, ba, q_b, k_b, v_b = _in_proj(x2, norm1_w.reshape(1, d), w_all, tm=512)

    lane_pad = (DN_HEADS, LANES - 2 * DN_HEADS)
    alog_row = jnp.pad(a_log.astype(F32), lane_pad).reshape(1, LANES)
    dtb_row = jnp.pad(dt_bias.astype(F32), lane_pad).reshape(1, LANES)
    dn_w, dn_qd, dn_kd, dn_u, dn_qk, dn_cd = _dn_prep(
        qkva.reshape(b, s, 3 * DN_WIDTH), ba.reshape(b, s, LANES), conv_w, alog_row, dtb_row)
    o_a = _dn_scan(dn_w, dn_qd, dn_kd, dn_u, dn_qk, dn_cd, z_a.reshape(b, s, DN_WIDTH),
                   dn_norm_w.reshape(1, DN_HEAD_DIM))
    o_b = _band_attn(q_b.reshape(b, s, CA_WIDTH), k_b.reshape(b, s, CA_WIDTH),
                     v_b.reshape(b, s, CA_WIDTH), _rel_bias_rows(rel_bias),
                     attn_norm_w.reshape(1, CA_WIDTH))

    wr_pad = jnp.pad(w_router.astype(F32), ((0, 0), (0, LANES - N_EXPERTS)))
    br_pad = jnp.pad(b_router.astype(F32), (0, LANES - N_EXPERTS)).reshape(1, LANES)
    x1, h2, route, gates, counts = _out_router(
        o_a.reshape(t, DN_WIDTH), o_b.reshape(t, CA_WIDTH), x2, w_out.astype(BF16),
        norm2_w.reshape(1, d), wr_pad, br_pad, tm=512)

    bm = EXPERT_BLOCK_ROWS
    n_assign = t * TOP_K
    n_blocks = n_assign // bm + N_EXPERTS
    n_slots = n_blocks * bm
    counts_i = counts[0, :N_EXPERTS].astype(jnp.int32)
    padded = (counts_i + bm - 1) // bm * bm
    pad_end = jnp.cumsum(padded)
    pad_start = pad_end - padded
    top_e = route[:, :TOP_K]
    expert_ids = jnp.arange(N_EXPERTS, dtype=jnp.int32)
    start_of = jnp.sum(jnp.where(top_e[..., None] == expert_ids, pad_start, 0), axis=-1)
    pos = start_of + route[:, TOP_K:2 * TOP_K]
    assign = jnp.arange(n_assign, dtype=jnp.int32)
    slot_assign = jnp.full((n_slots,), -1, jnp.int32).at[pos.reshape(-1)].set(
        assign, unique_indices=True, mode='promise_in_bounds')
    n_out_rows = n_assign + bm
    slot_row = jnp.arange(n_slots, dtype=jnp.int32) % bm
    is_real = slot_assign >= 0
    slot_src = jnp.where(is_real, slot_assign // TOP_K, 0)
    slot_dst = jnp.where(is_real, (slot_assign % TOP_K) * t + slot_assign // TOP_K, n_assign + slot_row)
    block_start = jnp.arange(n_blocks, dtype=jnp.int32) * bm
    block_e = jnp.minimum(jnp.sum((pad_end[None, :] <= block_start[:, None]).astype(jnp.int32), axis=1),
                          N_EXPERTS - 1)

    f = w_gate.shape[-1]
    ys = _experts(h2, slot_src, slot_dst, block_e, n_out_rows,
                  w_gate, b_gate.reshape(N_EXPERTS, 1, f), w_up, b_up.reshape(N_EXPERTS, 1, f),
                  w_down, b_down.reshape(N_EXPERTS, 1, d), bm)
    out = _combine(ys, x1, gates, out_norm_w.reshape(1, d), tm=256)
    return out.reshape(b, s, d)


def kernel(x, norm1_w, w_in, conv_w, a_log, dt_bias, dn_norm_w, rel_bias, attn_norm_w, w_out, norm2_w, w_router, b_router, w_gate, b_gate, w_up, b_up, w_down, b_down, final_norm_w):
    depth = norm1_w.shape[0]
    assert depth == 1, "the final RMSNorm is fused into the last layer's combine step"
    return _layer(x, norm1_w[0], w_in[0], conv_w[0], a_log[0], dt_bias[0], dn_norm_w[0],
                  rel_bias[0], attn_norm_w[0], w_out[0], norm2_w[0], w_router[0], b_router[0],
                  w_gate[0], b_gate[0], w_up[0], b_up[0], w_down[0], b_down[0], final_norm_w)
```

```python
import jax
import numpy as np
import jax.numpy as jnp
from jax import lax
from jax.experimental import pallas as pl
from jax.experimental.pallas import tpu as pltpu

F32 = jnp.float32
BF16 = jnp.bfloat16
HIGHEST = lax.Precision.HIGHEST

NORM_EPS = 1e-6
CHUNK = 64
DN_HEADS = 4
DN_HEAD_DIM = 128
DN_WIDTH = DN_HEADS * DN_HEAD_DIM
CONV_WIDTH = 4
CA_HEADS = 8
CA_HEAD_DIM = 64
CA_WIDTH = CA_HEADS * CA_HEAD_DIM
CA_LEFT_CHUNKS = 8
MAX_REL_DIST = 256
N_EXPERTS = 32
TOP_K = 4
SWIGLU_LIMIT = 7.0
SWIGLU_ALPHA = 1.702

LANES = 128
SUBLANES = 8
INV_BLOCK = 16
PREP_ROWS = 256
SCAN_GROUP = 8
ATT_ROWS = 256
ATT_WIN = ATT_ROWS + CA_LEFT_CHUNKS * CHUNK
ATT_EXT = 1024
EXPERT_BLOCK_ROWS = 256
N_DMA_PRIORITIES = 2
VMEM_LIMIT = 48 * 1024 * 1024


def _dot(a, b, dims=(((1,), (0,)), ((), ())), precision=None):
    return lax.dot_general(a, b, dims, precision=precision, preferred_element_type=F32)


def _dot_nt(a, b, precision=None):
    return _dot(a, b, (((1,), (1,)), ((), ())), precision)


def _bdot(a, b):
    return _dot(a.astype(BF16), b.astype(BF16))


def _dot_split(a, b, b_hi):
    a_hi = a.astype(BF16)
    a_lo = (a - a_hi.astype(F32)).astype(BF16)
    b_lo = (b - b_hi.astype(F32)).astype(BF16)
    return _dot(a_hi, b_hi) + (_dot(a_hi, b_lo) + _dot(a_lo, b_hi))


def _store_row_slabs(ref, value, rows):
    for j in range(SUBLANES):
        ref[pl.ds(j, rows, stride=SUBLANES), :] = value[:, j * LANES:(j + 1) * LANES]


def _load_row_slabs(ref, rows):
    return [ref[pl.ds(j, rows, stride=SUBLANES), :] for j in range(SUBLANES)]


def _sigmoid(x):
    return 1.0 / (1.0 + jnp.exp(-x))


def _silu(x):
    return x * _sigmoid(x)


def _in_proj_kernel(x_ref, nw_ref, w_ref, qkva_ref, z_ref, ba_ref, qb_ref, kb_ref, vb_ref):
    x = x_ref[...]
    h = x * lax.rsqrt(jnp.mean(x * x, axis=-1, keepdims=True) + NORM_EPS) * nw_ref[...]
    hb = h.astype(BF16)
    c0 = 3 * DN_WIDTH
    c1 = c0 + DN_WIDTH
    c2 = c1 + CA_WIDTH
    c3 = c2 + CA_WIDTH
    c4 = c3 + CA_WIDTH
    qkva_ref[...] = _dot(hb, w_ref[:, 0:c0])
    z_ref[...] = _dot(hb, w_ref[:, c0:c1])
    qb_ref[...] = _dot(hb, w_ref[:, c1:c2]).astype(BF16)
    kb_ref[...] = _dot(hb, w_ref[:, c2:c3]).astype(BF16)
    vb_ref[...] = _dot(hb, w_ref[:, c3:c4]).astype(BF16)
    ba_ref[...] = _dot(hb, w_ref[:, c4:c4 + LANES])


def _in_proj(x2, norm_w, w_all, tm):
    t, d = x2.shape
    wcols = w_all.shape[1]
    row = lambda i: (i, 0)
    fixed = lambda i: (0, 0)
    return pl.pallas_call(
        _in_proj_kernel,
        grid=(t // tm,),
        in_specs=[
            pl.BlockSpec((tm, d), row),
            pl.BlockSpec((1, d), fixed),
            pl.BlockSpec((d, wcols), fixed),
        ],
        out_specs=[
            pl.BlockSpec((tm, 3 * DN_WIDTH), row),
            pl.BlockSpec((tm, DN_WIDTH), row),
            pl.BlockSpec((tm, LANES), row),
            pl.BlockSpec((tm, CA_WIDTH), row),
            pl.BlockSpec((tm, CA_WIDTH), row),
            pl.BlockSpec((tm, CA_WIDTH), row),
        ],
        out_shape=[
            jax.ShapeDtypeStruct((t, 3 * DN_WIDTH), F32),
            jax.ShapeDtypeStruct((t, DN_WIDTH), F32),
            jax.ShapeDtypeStruct((t, LANES), F32),
            jax.ShapeDtypeStruct((t, CA_WIDTH), BF16),
            jax.ShapeDtypeStruct((t, CA_WIDTH), BF16),
            jax.ShapeDtypeStruct((t, CA_WIDTH), BF16),
        ],
        compiler_params=pltpu.CompilerParams(
            dimension_semantics=("parallel",), vmem_limit_bytes=VMEM_LIMIT),
        name="in_proj",
    )(x2, norm_w, w_all)


def _unit_lower_inverses(lowers, eye, inv_block):
    eye16 = eye.astype(BF16)
    bf = lambda a, b: _dot(a, b).astype(BF16)
    each = lambda fn, *lists: [fn(*args) for args in zip(*lists)]
    diag = each(lambda l: jnp.where(inv_block, l, 0.0), lowers)
    diag16 = each(lambda x: x.astype(BF16), diag)
    off16 = each(lambda l, x: (l - x).astype(BF16), lowers, diag)
    d2 = each(lambda x: bf(x, x), diag16)
    d4 = each(lambda x: bf(x, x), d2)
    d8 = each(lambda x: bf(x, x), d4)
    pa = each(lambda x, y: bf(eye16 - x, eye16 + y), diag16, d2)
    pb = each(lambda x, y: bf(eye16 + x, eye16 + y), d4, d8)
    p = each(bf, pa, pb)
    m = each(bf, p, off16)
    m2 = each(lambda x: bf(x, x), m)
    mm = each(lambda x, y: bf(eye16 - x, eye16 + y), m, m2)
    t0 = each(_dot, mm, p)
    t0_hi = each(lambda x: x.astype(BF16), t0)
    resid = each(lambda l, x, x_hi: (eye - x) - _dot_split(l, x, x_hi), lowers, t0, t0_hi)
    return each(lambda x, x_hi, res: x + _dot(x_hi, res.astype(BF16)), t0, t0_hi, resid)


def _dn_prep_kernel(cur_ref, halo_ref, ba_ref, convw_ref, alog_ref, dtb_ref,
                    w_ref, qd_ref, kd_ref, u_ref, qk_ref, cd_ref):
    n = pl.program_id(1)
    r = PREP_ROWS
    cur = cur_ref[0]
    halo = jnp.where(n > 0, halo_ref[0], 0.0)
    full = jnp.concatenate([halo, cur], axis=0)
    conv = full[SUBLANES:] * convw_ref[CONV_WIDTH - 1:CONV_WIDTH, :]
    for j in range(CONV_WIDTH - 1):
        shift = CONV_WIDTH - 1 - j
        conv = conv + pltpu.roll(full, shift, axis=0)[SUBLANES:] * convw_ref[j:j + 1, :]
    qkv = _silu(conv)

    ba = ba_ref[0]
    beta_all = _sigmoid(ba)
    sp_in = ba + dtb_ref[...]
    softplus = jnp.maximum(sp_in, 0.0) + jnp.log1p(jnp.exp(-jnp.abs(sp_in)))
    g_all = -jnp.exp(alog_ref[...]) * softplus
    rows = lax.broadcasted_iota(jnp.int32, (r, r), 0)
    cols = lax.broadcasted_iota(jnp.int32, (r, r), 1)
    same_chunk = (rows // CHUNK) == (cols // CHUNK)
    causal = same_chunk & (rows >= cols)
    strict = same_chunk & (rows > cols)
    inv_block = (rows // INV_BLOCK) == (cols // INV_BLOCK)
    eye = (rows == cols).astype(F32)
    gc_all = _dot(causal.astype(F32), g_all, precision=HIGHEST)
    gc_last_all = _dot(same_chunk.astype(F32), g_all, precision=HIGHEST)
    gc_rows = gc_all.T
    chunk_decay = jnp.exp(gc_last_all)
    cd_ref[0] = jnp.concatenate(
        [chunk_decay[c * CHUNK:c * CHUNK + SUBLANES] for c in range(r // CHUNK)], axis=0)

    scale = DN_HEAD_DIM ** -0.5
    lowers, rhss, qk_parts = [], [], []
    for h in range(DN_HEADS):
        lo = h * DN_HEAD_DIM
        q = qkv[:, lo:lo + DN_HEAD_DIM]
        k = qkv[:, DN_WIDTH + lo:DN_WIDTH + lo + DN_HEAD_DIM]
        v = qkv[:, 2 * DN_WIDTH + lo:2 * DN_WIDTH + lo + DN_HEAD_DIM]
        q = q * lax.rsqrt(jnp.sum(q * q, axis=-1, keepdims=True) + NORM_EPS) * scale
        k = k * lax.rsqrt(jnp.sum(k * k, axis=-1, keepdims=True) + NORM_EPS)
        beta = beta_all[:, h:h + 1]
        gc = gc_all[:, DN_HEADS + h:DN_HEADS + h + 1]
        gc_row = gc_rows[DN_HEADS + h:DN_HEADS + h + 1, :]
        gc_last = gc_last_all[:, DN_HEADS + h:DN_HEADS + h + 1]
        decay = jnp.where(causal, jnp.exp(jnp.where(causal, gc - gc_row, 0.0)), 0.0)
        egc = jnp.exp(gc)
        k_beta = k * beta
        k16 = k.astype(BF16)
        lowers.append(jnp.where(strict, _dot_nt(k_beta.astype(BF16), k16) * decay, 0.0))
        rhss.append(jnp.concatenate([v * beta, k_beta * egc], axis=-1).astype(BF16))
        qd_ref[0, :, lo:lo + DN_HEAD_DIM] = (q * egc).astype(BF16)
        kd_ref[0, :, lo:lo + DN_HEAD_DIM] = k * jnp.exp(gc_last - gc)
        qk = _dot_nt(q.astype(BF16), k16) * decay
        compact = qk[:, 0:CHUNK]
        for c in range(1, r // CHUNK):
            compact = compact + qk[:, c * CHUNK:(c + 1) * CHUNK]
        qk_parts.append(compact.astype(BF16))
    qk_ref[0] = jnp.concatenate(qk_parts, axis=-1)

    t_invs = _unit_lower_inverses(lowers, eye, inv_block)
    for h in range(DN_HEADS):
        lo = h * DN_HEAD_DIM
        sol = _dot(t_invs[h].astype(BF16), rhss[h])
        u_ref[0, :, lo:lo + DN_HEAD_DIM] = sol[:, :DN_HEAD_DIM]
        w_ref[0, :, lo:lo + DN_HEAD_DIM] = sol[:, DN_HEAD_DIM:].astype(BF16)


def _dn_prep(qkva, ba, conv_w, alog_row, dtb_row):
    b, s, _ = qkva.shape
    r = PREP_ROWS
    n_chunks = s // CHUNK
    halo_blocks = r // SUBLANES
    fixed = lambda i, n: (0, 0)
    blk = lambda i, n: (i, n, 0)
    return pl.pallas_call(
        _dn_prep_kernel,
        grid=(b, s // r),
        in_specs=[
            pl.BlockSpec((1, r, 3 * DN_WIDTH), blk),
            pl.BlockSpec((1, SUBLANES, 3 * DN_WIDTH),
                         lambda i, n: (i, jnp.maximum(n * halo_blocks - 1, 0), 0)),
            pl.BlockSpec((1, r, LANES), blk),
            pl.BlockSpec((CONV_WIDTH, 3 * DN_WIDTH), fixed),
            pl.BlockSpec((1, LANES), fixed),
            pl.BlockSpec((1, LANES), fixed),
        ],
        out_specs=[
            pl.BlockSpec((1, r, DN_WIDTH), blk),
            pl.BlockSpec((1, r, DN_WIDTH), blk),
            pl.BlockSpec((1, r, DN_WIDTH), blk),
            pl.BlockSpec((1, r, DN_WIDTH), blk),
            pl.BlockSpec((1, r, DN_HEADS * CHUNK), blk),
            pl.BlockSpec((1, r // CHUNK * SUBLANES, LANES), blk),
        ],
        out_shape=[
            jax.ShapeDtypeStruct((b, s, DN_WIDTH), BF16),
            jax.ShapeDtypeStruct((b, s, DN_WIDTH), BF16),
            jax.ShapeDtypeStruct((b, s, DN_WIDTH), F32),
            jax.ShapeDtypeStruct((b, s, DN_WIDTH), F32),
            jax.ShapeDtypeStruct((b, s, DN_HEADS * CHUNK), BF16),
            jax.ShapeDtypeStruct((b, n_chunks * SUBLANES, LANES), F32),
        ],
        compiler_params=pltpu.CompilerParams(
            dimension_semantics=("parallel", "parallel"), vmem_limit_bytes=VMEM_LIMIT),
        name="dn_prep",
    )(qkva, qkva, ba, conv_w, alog_row, dtb_row)


def _dn_scan_kernel(w_ref, qd_ref, kd_ref, u_ref, qk_ref, cd_ref, z_ref, normw_ref, o_ref, state_ref):
    n = pl.program_id(0)

    @pl.when(n == 0)
    def _():
        state_ref[...] = jnp.zeros_like(state_ref)

    n_batch = w_ref.shape[0]
    chains = [(bi, h) for bi in range(n_batch) for h in range(DN_HEADS)]
    for g in range(0, len(chains), SCAN_GROUP):
        group = chains[g:g + SCAN_GROUP]
        col = lambda h: slice(h * DN_HEAD_DIM, (h + 1) * DN_HEAD_DIM)
        s16 = [state_ref[bi * DN_HEADS + h].astype(BF16) for bi, h in group]
        ws = [_dot(w_ref[bi, :, col(h)], s) for (bi, h), s in zip(group, s16)]
        qs = [_dot(qd_ref[bi, :, col(h)], s) for (bi, h), s in zip(group, s16)]
        vn16 = [(u_ref[bi, :, col(h)] - x).astype(BF16) for (bi, h), x in zip(group, ws)]
        inner = [_dot(qk_ref[bi, :, h * CHUNK:(h + 1) * CHUNK], v) for (bi, h), v in zip(group, vn16)]
        kd_t = [kd_ref[bi, :, col(h)].T.astype(BF16) for bi, h in group]
        upd = [_dot(k, v) for k, v in zip(kd_t, vn16)]
        for (bi, h), x in zip(group, upd):
            cd = cd_ref[bi, 0:1, DN_HEADS + h:DN_HEADS + h + 1]
            state_ref[bi * DN_HEADS + h] = state_ref[bi * DN_HEADS + h] * cd + x
        for (bi, h), a, c in zip(group, qs, inner):
            o = a + c
            o = o * lax.rsqrt(jnp.mean(o * o, axis=-1, keepdims=True) + NORM_EPS) * normw_ref[...]
            o_ref[bi, :, col(h)] = o * _silu(z_ref[bi, :, col(h)])


def _dn_scan(w, qd, kd, u, qk, cd, z, norm_w):
    b, s, _ = w.shape
    n_chunks = s // CHUNK
    blk = lambda n: (0, n, 0)
    wide = pl.BlockSpec((b, CHUNK, DN_WIDTH), blk)
    return pl.pallas_call(
        _dn_scan_kernel,
        grid=(n_chunks,),
        in_specs=[
            wide, wide, wide, wide,
            pl.BlockSpec((b, CHUNK, DN_HEADS * CHUNK), blk),
            pl.BlockSpec((b, SUBLANES, LANES), blk),
            wide,
            pl.BlockSpec((1, DN_HEAD_DIM), lambda n: (0, 0)),
        ],
        out_specs=wide,
        out_shape=jax.ShapeDtypeStruct((b, s, DN_WIDTH), F32),
        scratch_shapes=[pltpu.VMEM((b * DN_HEADS, DN_HEAD_DIM, DN_HEAD_DIM), F32)],
        compiler_params=pltpu.CompilerParams(
            dimension_semantics=("arbitrary",), vmem_limit_bytes=VMEM_LIMIT),
        name="dn_scan",
    )(w, qd, kd, u, qk, cd, z, norm_w)


def _band_attn_kernel(q_ref, k_ref, v_ref, ext_ref, nw_ref, o_ref, bias_ref):
    first = jnp.logical_and(pl.program_id(0) == 0, pl.program_id(1) == 0)

    @pl.when(first)
    def _():
        qi = lax.broadcasted_iota(jnp.int32, (ATT_ROWS, ATT_WIN), 0) // CHUNK
        kj = lax.broadcasted_iota(jnp.int32, (ATT_ROWS, ATT_WIN), 1) // CHUNK
        in_band = jnp.logical_and(kj >= qi, kj <= qi + CA_LEFT_CHUNKS)
        for h in range(CA_HEADS):
            tiled = jnp.broadcast_to(ext_ref[h:h + 1, :], (ATT_ROWS, ATT_EXT))
            toeplitz = pltpu.roll(tiled, 0, axis=1, stride=1, stride_axis=0)
            bias_ref[h] = jnp.where(in_band, toeplitz[:, :ATT_WIN], -jnp.inf)

    n0 = pl.program_id(1) * (ATT_ROWS // CHUNK)
    k_parts, v_parts = [], []
    for w in range(ATT_WIN // CHUNK):
        start = pl.multiple_of(jnp.maximum(n0 - CA_LEFT_CHUNKS + w, 0) * CHUNK, CHUNK)
        k_parts.append(k_ref[0, pl.ds(start, CHUNK), :])
        v_parts.append(v_ref[0, pl.ds(start, CHUNK), :])
    k_win = jnp.concatenate(k_parts, axis=0)
    v_win = jnp.concatenate(v_parts, axis=0)
    q = q_ref[0] * jnp.asarray(CA_HEAD_DIM ** -0.5, BF16)

    key_col = lax.broadcasted_iota(jnp.int32, (1, ATT_WIN), 1)
    before_start = key_col < (CA_LEFT_CHUNKS - n0) * CHUNK
    col_bias = jnp.where(before_start, -jnp.inf, 0.0)
    lane = lax.broadcasted_iota(jnp.int32, (ATT_ROWS, LANES), 1)
    heads_per_group = LANES // CA_HEAD_DIM

    outs = []
    for grp in range(CA_WIDTH // LANES):
        lo = grp * LANES
        qg = q[:, lo:lo + LANES]
        kg = k_win[:, lo:lo + LANES]
        vg = v_win[:, lo:lo + LANES]
        og = jnp.zeros((ATT_ROWS, LANES), F32)
        for hh in range(heads_per_group):
            head = grp * heads_per_group + hh
            in_head = (lane // CA_HEAD_DIM) == hh
            qm = jnp.where(in_head, qg, jnp.zeros_like(qg))
            s = _dot_nt(qm, kg) + bias_ref[head] + col_bias
            e = jnp.exp(s - jnp.max(s, axis=-1, keepdims=True))
            denom = jnp.sum(e, axis=-1, keepdims=True)
            og = jnp.where(in_head, _dot(e.astype(BF16), vg) / denom, og)
        outs.append(og)
    o = jnp.concatenate(outs, axis=-1)
    o = o * lax.rsqrt(jnp.mean(o * o, axis=-1, keepdims=True) + NORM_EPS) * nw_ref[...]
    o_ref[0] = o


def _band_attn(qb, kb, vb, ext, norm_w):
    b, s, _ = qb.shape
    return pl.pallas_call(
        _band_attn_kernel,
        grid=(b, s // ATT_ROWS),
        in_specs=[
            pl.BlockSpec((1, ATT_ROWS, CA_WIDTH), lambda i, n: (i, n, 0)),
            pl.BlockSpec((1, s, CA_WIDTH), lambda i, n: (i, 0, 0)),
            pl.BlockSpec((1, s, CA_WIDTH), lambda i, n: (i, 0, 0)),
            pl.BlockSpec((CA_HEADS, ATT_EXT), lambda i, n: (0, 0)),
            pl.BlockSpec((1, CA_WIDTH), lambda i, n: (0, 0)),
        ],
        out_specs=pl.BlockSpec((1, ATT_ROWS, CA_WIDTH), lambda i, n: (i, n, 0)),
        out_shape=jax.ShapeDtypeStruct((b, s, CA_WIDTH), F32),
        scratch_shapes=[pltpu.VMEM((CA_HEADS, ATT_ROWS, ATT_WIN), F32)],
        compiler_params=pltpu.CompilerParams(
            dimension_semantics=("arbitrary", "arbitrary"), vmem_limit_bytes=VMEM_LIMIT),
        name="band_attn",
    )(qb, kb, vb, ext, norm_w)


def _out_router_kernel(oa_ref, ob_ref, x_ref, wout_ref, n2_ref, wr_hi_ref, wr_lo_ref, br_ref,
                       x1_ref, h2_ref, route_ref, gates_ref, counts_ref, run_ref, earlier_ref):
    i = pl.program_id(0)
    tm = x_ref.shape[0]

    @pl.when(i == 0)
    def _():
        run_ref[...] = jnp.zeros_like(run_ref)
        r = lax.broadcasted_iota(jnp.int32, (tm, tm), 0)
        c = lax.broadcasted_iota(jnp.int32, (tm, tm), 1)
        earlier_ref[...] = (r > c).astype(BF16)

    x1 = (x_ref[...]
          + _dot(oa_ref[...].astype(BF16), wout_ref[0:DN_WIDTH, :])
          + _dot(ob_ref[...].astype(BF16), wout_ref[DN_WIDTH:, :]))
    x1_ref[...] = x1
    h2 = x1 * lax.rsqrt(jnp.mean(x1 * x1, axis=-1, keepdims=True) + NORM_EPS) * n2_ref[...]
    _store_row_slabs(h2_ref, h2, tm)
    lane = lax.broadcasted_iota(jnp.int32, (tm, LANES), 1)
    h2_hi = h2.astype(BF16)
    h2_lo = (h2 - h2_hi.astype(F32)).astype(BF16)
    logits = (_dot(h2_hi, wr_hi_ref[...]) + (_dot(h2_hi, wr_lo_ref[...]) + _dot(h2_lo, wr_hi_ref[...]))
              + br_ref[...])
    logits = jnp.where(lane < N_EXPERTS, logits, -jnp.inf)

    top_vals, top_idx, onehots = [], [], []
    for _ in range(TOP_K):
        m = jnp.max(logits, axis=-1, keepdims=True)
        idx = jnp.min(jnp.where(logits == m, lane, LANES), axis=-1, keepdims=True)
        hit = lane == idx
        top_vals.append(m)
        top_idx.append(idx)
        onehots.append(hit)
        logits = jnp.where(hit, -jnp.inf, logits)

    exps = [jnp.exp(v - top_vals[0]) for v in top_vals]
    denom = exps[0] + exps[1] + exps[2] + exps[3]
    gates = jnp.zeros((tm, LANES), F32)
    for k in range(TOP_K):
        gates = jnp.where(lane == k, exps[k] / denom, gates)
    gates_ref[...] = gates

    multi = jnp.zeros((tm, LANES), F32)
    for hit in onehots:
        multi = multi + hit.astype(F32)
    before = _dot(earlier_ref[...], multi.astype(BF16)) + run_ref[...]
    route = jnp.zeros((tm, LANES), jnp.int32)
    for k in range(TOP_K):
        rank = jnp.sum(jnp.where(onehots[k], before, 0.0), axis=-1, keepdims=True).astype(jnp.int32)
        route = jnp.where(lane == k, top_idx[k], route)
        route = jnp.where(lane == TOP_K + k, rank, route)
    route_ref[...] = route
    run_ref[...] = run_ref[...] + jnp.sum(multi, axis=0, keepdims=True)
    counts_ref[...] = run_ref[...]


def _out_router(oa, ob, x2, w_out, norm2_w, wr_hi, wr_lo, br_pad, tm):
    t, d = x2.shape
    row = lambda i: (i, 0)
    fixed = lambda i: (0, 0)
    return pl.pallas_call(
        _out_router_kernel,
        grid=(t // tm,),
        in_specs=[
            pl.BlockSpec((tm, DN_WIDTH), row),
            pl.BlockSpec((tm, CA_WIDTH), row),
            pl.BlockSpec((tm, d), row),
            pl.BlockSpec((DN_WIDTH + CA_WIDTH, d), fixed),
            pl.BlockSpec((1, d), fixed),
            pl.BlockSpec((d, LANES), fixed),
            pl.BlockSpec((d, LANES), fixed),
            pl.BlockSpec((1, LANES), fixed),
        ],
        out_specs=[
            pl.BlockSpec((tm, d), row),
            pl.BlockSpec((tm * SUBLANES, LANES), row),
            pl.BlockSpec((tm, LANES), row),
            pl.BlockSpec((tm, LANES), row),
            pl.BlockSpec((1, LANES), fixed),
        ],
        out_shape=[
            jax.ShapeDtypeStruct((t, d), F32),
            jax.ShapeDtypeStruct((t * SUBLANES, LANES), F32),
            jax.ShapeDtypeStruct((t, LANES), jnp.int32),
            jax.ShapeDtypeStruct((t, LANES), F32),
            jax.ShapeDtypeStruct((1, LANES), F32),
        ],
        scratch_shapes=[pltpu.VMEM((1, LANES), F32), pltpu.VMEM((tm, tm), BF16)],
        compiler_params=pltpu.CompilerParams(
            dimension_semantics=("arbitrary",), vmem_limit_bytes=VMEM_LIMIT),
        name="out_router",
    )(oa, ob, x2, w_out, norm2_w, wr_hi, wr_lo, br_pad)


def _experts_kernel(be_ref, src0_ref, src_next_ref, dst_prev_ref, dst_cur_ref, h2_ref,
                    wg_ref, bg_ref, wu_ref, bu_ref, wd_ref, bd_ref, g_ref,
                    xa, xb, ya, yb, wg16, wu16, wd16, sem_in, sem_out):
    b = pl.program_id(0)
    last = pl.num_programs(0) - 1
    slab = SUBLANES
    bm = xa.shape[0] // slab
    xbufs = (xa, xb)
    ybufs = (ya, yb)

    def gather_rows(idx_ref, slot):
        for r in range(bm):
            src = pl.multiple_of(idx_ref[0, 0, r], slab)
            pltpu.make_async_copy(h2_ref.at[pl.ds(src, slab)],
                                  xbufs[slot].at[pl.ds(r * slab, slab)], sem_in.at[slot]
                                  ).start(priority=r % N_DMA_PRIORITIES)

    def scatter_rows(idx_ref, slot):
        for r in range(bm):
            dst = pl.multiple_of(idx_ref[0, 0, r], slab)
            pltpu.make_async_copy(ybufs[slot].at[pl.ds(r * slab, slab)],
                                  g_ref.at[pl.ds(dst, slab)], sem_out.at[slot]
                                  ).start(priority=r % N_DMA_PRIORITIES)

    def wait_gather(slot):
        pltpu.make_async_copy(h2_ref.at[pl.ds(0, bm * slab)], xbufs[slot], sem_in.at[slot]).wait()

    def wait_scatter(slot):
        pltpu.make_async_copy(ybufs[slot], g_ref.at[pl.ds(0, bm * slab)], sem_out.at[slot]).wait()

    @pl.when(b == 0)
    def _():
        gather_rows(src0_ref, 0)
        yb[...] = jnp.zeros(yb.shape, yb.dtype)

    changed = jnp.logical_or(b == 0, be_ref[b] != be_ref[jnp.maximum(b - 1, 0)])

    @pl.when(changed)
    def _():
        wg16[...] = wg_ref[0].astype(BF16)
        wu16[...] = wu_ref[0].astype(BF16)
        wd16[...] = wd_ref[0].astype(BF16)

    def step(cur):
        oth = 1 - cur
        wait_gather(cur)
        gather_rows(src_next_ref, oth)
        scatter_rows(dst_prev_ref, oth)
        x = jnp.concatenate(_load_row_slabs(xbufs[cur], bm), axis=-1).astype(BF16)
        gate = _dot(x, wg16[...]) + bg_ref[0]
        up = _dot(x, wu16[...]) + bu_ref[0]
        gate = jnp.minimum(gate, SWIGLU_LIMIT)
        up = jnp.clip(up, -SWIGLU_LIMIT, SWIGLU_LIMIT)
        glu = gate * _sigmoid(gate * SWIGLU_ALPHA)
        mid = ((up + 1.0) * glu).astype(BF16)
        _store_row_slabs(ybufs[cur], _dot(mid, wd16[...]) + bd_ref[0], bm)
        wait_scatter(oth)

        @pl.when(b == last)
        def _():
            scatter_rows(dst_cur_ref, cur)
            wait_scatter(cur)
            wait_gather(oth)

    @pl.when(b % 2 == 0)
    def _():
        step(0)

    @pl.when(b % 2 == 1)
    def _():
        step(1)


def _experts(h2, slot_src, slot_dst, block_e, n_out_rows, wg, bg, wu, bu, wd, bd, bm):
    slab = SUBLANES
    d = wg.shape[1]
    f = wg.shape[2]
    n_blocks = slot_src.shape[0] // bm
    spare = n_out_rows - bm + jnp.arange(bm, dtype=jnp.int32)
    src3 = (slot_src * slab).reshape(n_blocks, 1, bm)
    dst3 = (jnp.concatenate([spare, slot_dst]) * slab).reshape(n_blocks + 1, 1, bm)
    smem_blk = lambda imap: pl.BlockSpec((1, 1, bm), imap, memory_space=pltpu.SMEM)
    wmap = lambda b, be: (be[b], 0, 0)
    grid_spec = pltpu.PrefetchScalarGridSpec(
        num_scalar_prefetch=1,
        grid=(n_blocks,),
        in_specs=[
            smem_blk(lambda b, be: (0, 0, 0)),
            smem_blk(lambda b, be: (jnp.minimum(b + 1, n_blocks - 1), 0, 0)),
            smem_blk(lambda b, be: (b, 0, 0)),
            smem_blk(lambda b, be: (b + 1, 0, 0)),
            pl.BlockSpec(memory_space=pl.ANY),
            pl.BlockSpec((1, d, f), wmap),
            pl.BlockSpec((1, 1, f), wmap),
            pl.BlockSpec((1, d, f), wmap),
            pl.BlockSpec((1, 1, f), wmap),
            pl.BlockSpec((1, f, d), wmap),
            pl.BlockSpec((1, 1, d), wmap),
        ],
        out_specs=pl.BlockSpec(memory_space=pl.ANY),
        scratch_shapes=[
            pltpu.VMEM((bm * slab, LANES), F32),
            pltpu.VMEM((bm * slab, LANES), F32),
            pltpu.VMEM((bm * slab, LANES), F32),
            pltpu.VMEM((bm * slab, LANES), F32),
            pltpu.VMEM((d, f), BF16),
            pltpu.VMEM((d, f), BF16),
            pltpu.VMEM((f, d), BF16),
            pltpu.SemaphoreType.DMA((2,)),
            pltpu.SemaphoreType.DMA((2,)),
        ],
    )
    return pl.pallas_call(
        _experts_kernel,
        grid_spec=grid_spec,
        out_shape=jax.ShapeDtypeStruct((n_out_rows * slab, LANES), F32),
        compiler_params=pltpu.CompilerParams(
            dimension_semantics=("arbitrary",), vmem_limit_bytes=VMEM_LIMIT),
        name="experts",
    )(block_e, src3, src3, dst3, dst3, h2, wg, bg, wu, bu, wd, bd)


def _combine_kernel(y0_ref, y1_ref, y2_ref, y3_ref, x1_ref, gates_ref, fw_ref, o_ref):
    tm = x1_ref.shape[0]
    acc = x1_ref[...]
    gates = gates_ref[...]
    for k, y_ref in enumerate((y0_ref, y1_ref, y2_ref, y3_ref)):
        y = jnp.concatenate(_load_row_slabs(y_ref, tm), axis=-1)
        acc = acc + y * gates[:, k:k + 1]
    o_ref[...] = acc * lax.rsqrt(jnp.mean(acc * acc, axis=-1, keepdims=True) + NORM_EPS) * fw_ref[...]


def _combine(ys, x1, gates, final_w, tm):
    t, d = x1.shape
    n_tiles = t // tm
    row = lambda i: (i, 0)
    y_spec = lambda k: pl.BlockSpec((tm * SUBLANES, LANES), lambda i: (k * n_tiles + i, 0))
    return pl.pallas_call(
        _combine_kernel,
        grid=(n_tiles,),
        in_specs=[y_spec(k) for k in range(TOP_K)] + [
            pl.BlockSpec((tm, d), row),
            pl.BlockSpec((tm, LANES), row),
            pl.BlockSpec((1, d), lambda i: (0, 0)),
        ],
        out_specs=pl.BlockSpec((tm, d), row),
        out_shape=jax.ShapeDtypeStruct((t, d), F32),
        compiler_params=pltpu.CompilerParams(
            dimension_semantics=("parallel",), vmem_limit_bytes=VMEM_LIMIT),
        name="combine",
    )(ys, ys, ys, ys, x1, gates, final_w)


def _rel_bias_rows(rel_bias):
    m = np.arange(ATT_EXT)
    m = np.where(m < ATT_WIN, m, m - ATT_EXT)
    dist = CA_LEFT_CHUNKS * CHUNK - m
    idx = np.clip(dist, -MAX_REL_DIST, MAX_REL_DIST) + MAX_REL_DIST
    return rel_bias.astype(F32)[:, idx]


def _layer(x, norm1_w, w_in, conv_w, a_log, dt_bias, dn_norm_w, rel_bias, attn_norm_w, w_out,
           norm2_w, w_router, b_router, w_gate, b_gate, w_up, b_up, w_down, b_down, out_norm_w):
    b, s, d = x.shape
    assert d == SUBLANES * LANES, "row-slab layout holds one model row per (8, 128) tile"
    assert s % ATT_ROWS == 0 and s % PREP_ROWS == 0
    t = b * s
    x2 = x.reshape(t, d)

    ba_lo = 4 * DN_WIDTH
    ba_hi = ba_lo + 2 * DN_HEADS
    w_small = jnp.pad(w_in[:, ba_lo:ba_hi], ((0, 0), (0, LANES - 2 * DN_HEADS)))
    w_all = jnp.concatenate([w_in[:, :ba_lo], w_in[:, ba_hi:], w_small], axis=1).astype(BF16)
    qkva, z_a, ba, q_b, k_b, v_b = _in_proj(x2, norm1_w.reshape(1, d), w_all, tm=512)

    lane_pad = (DN_HEADS, LANES - 2 * DN_HEADS)
    alog_row = jnp.pad(a_log.astype(F32), lane_pad).reshape(1, LANES)
    dtb_row = jnp.pad(dt_bias.astype(F32), lane_pad).reshape(1, LANES)
    dn_w, dn_qd, dn_kd, dn_u, dn_qk, dn_cd = _dn_prep(
        qkva.reshape(b, s, 3 * DN_WIDTH), ba.reshape(b, s, LANES), conv_w, alog_row, dtb_row)
    o_a = _dn_scan(dn_w, dn_qd, dn_kd, dn_u, dn_qk, dn_cd, z_a.reshape(b, s, DN_WIDTH),
                   dn_norm_w.reshape(1, DN_HEAD_DIM))
    o_b = _band_attn(q_b.reshape(b, s, CA_WIDTH), k_b.reshape(b, s, CA_WIDTH),
                     v_b.reshape(b, s, CA_WIDTH), _rel_bias_rows(rel_bias),
                     attn_norm_w.reshape(1, CA_WIDTH))

    wr_pad = jnp.pad(w_router.astype(F32), ((0, 0), (0, LANES - N_EXPERTS)))
    wr_hi = wr_pad.astype(BF16)
    br_pad = jnp.pad(b_router.astype(F32), (0, LANES - N_EXPERTS)).reshape(1, LANES)
    x1, h2, route, gates, counts = _out_router(
        o_a.reshape(t, DN_WIDTH), o_b.reshape(t, CA_WIDTH), x2, w_out.astype(BF16),
        norm2_w.reshape(1, d), wr_hi, (wr_pad - wr_hi.astype(F32)).astype(BF16), br_pad, tm=512)

    bm = EXPERT_BLOCK_ROWS
    n_assign = t * TOP_K
    n_blocks = n_assign // bm + N_EXPERTS
    n_slots = n_blocks * bm
    counts_i = counts[0, :N_EXPERTS].astype(jnp.int32)
    padded = (counts_i + bm - 1) // bm * bm
    pad_end = jnp.cumsum(padded)
    pad_start = pad_end - padded
    top_e = route[:, :TOP_K]
    expert_ids = jnp.arange(N_EXPERTS, dtype=jnp.int32)
    start_of = jnp.sum(jnp.where(top_e[..., None] == expert_ids, pad_start, 0), axis=-1)
    pos = start_of + route[:, TOP_K:2 * TOP_K]
    assign = jnp.arange(n_assign, dtype=jnp.int32)
    slot_assign = jnp.full((n_slots,), -1, jnp.int32).at[pos.reshape(-1)].set(
        assign, unique_indices=True, mode='promise_in_bounds')
    n_out_rows = n_assign + bm
    slot_row = jnp.arange(n_slots, dtype=jnp.int32) % bm
    is_real = slot_assign >= 0
    slot_src = jnp.where(is_real, slot_assign // TOP_K, 0)
    slot_dst = jnp.where(is_real, (slot_assign % TOP_K) * t + slot_assign // TOP_K, n_assign + slot_row)
    block_start = jnp.arange(n_blocks, dtype=jnp.int32) * bm
    block_e = jnp.minimum(jnp.sum((pad_end[None, :] <= block_start[:, None]).astype(jnp.int32), axis=1),
                          N_EXPERTS - 1)

    f = w_gate.shape[-1]
    ys = _experts(h2, slot_src, slot_dst, block_e, n_out_rows,
                  w_gate, b_gate.reshape(N_EXPERTS, 1, f), w_up, b_up.reshape(N_EXPERTS, 1, f),
                  w_down, b_down.reshape(N_EXPERTS, 1, d), bm)
    out = _combine(ys, x1, gates, out_norm_w.reshape(1, d), tm=256)
    return out.reshape(b, s, d)


def kernel(x, norm1_w, w_in, conv_w, a_log, dt_bias, dn_norm_w, rel_bias, attn_norm_w, w_out, norm2_w, w_router, b_router, w_gate, b_gate, w_up, b_up, w_down, b_down, final_norm_w):
    depth = norm1_w.shape[0]
    assert depth == 1, "the final RMSNorm is fused into the last layer's combine step"
    return _layer(x, norm1_w[0], w_in[0], conv_w[0], a_log[0], dt_bias[0], dn_norm_w[0],
                  rel_bias[0], attn_norm_w[0], w_out[0], norm2_w[0], w_router[0], b_router[0],
                  w_gate[0], b_gate[0], w_up[0], b_up[0], w_down[0], b_down[0], final_norm_w)
```

```python
import functools

import jax
import numpy as np
import jax.numpy as jnp
from jax import lax
from jax.experimental import pallas as pl
from jax.experimental.pallas import tpu as pltpu

F32 = jnp.float32
BF16 = jnp.bfloat16
HIGHEST = lax.Precision.HIGHEST

NORM_EPS = 1e-6
CHUNK = 64
DN_HEADS = 4
DN_HEAD_DIM = 128
DN_WIDTH = DN_HEADS * DN_HEAD_DIM
CONV_WIDTH = 4
CA_HEADS = 8
CA_HEAD_DIM = 64
CA_WIDTH = CA_HEADS * CA_HEAD_DIM
CA_LEFT_CHUNKS = 8
MAX_REL_DIST = 256
N_EXPERTS = 32
TOP_K = 4
SWIGLU_LIMIT = 7.0
SWIGLU_ALPHA = 1.702

LANES = 128
SUBLANES = 8
INV_BLOCK = 16
PREP_ROWS = 256
SCAN_GROUP = 8
ATT_ROWS = 256
ATT_WIN = ATT_ROWS + CA_LEFT_CHUNKS * CHUNK
ATT_EXT = 1024
EXPERT_BLOCK_ROWS = 256
RING = 4
INVERT_CHUNK = 8192
INVERT_UNROLL = 8
VMEM_LIMIT = 48 * 1024 * 1024


def _dot(a, b, dims=(((1,), (0,)), ((), ())), precision=None):
    return lax.dot_general(a, b, dims, precision=precision, preferred_element_type=F32)


def _dot_nt(a, b, precision=None):
    return _dot(a, b, (((1,), (1,)), ((), ())), precision)


def _bdot(a, b):
    return _dot(a.astype(BF16), b.astype(BF16))


def _dot_split(a, b, b_hi):
    a_hi = a.astype(BF16)
    a_lo = (a - a_hi.astype(F32)).astype(BF16)
    b_lo = (b - b_hi.astype(F32)).astype(BF16)
    return _dot(a_hi, b_hi) + (_dot(a_hi, b_lo) + _dot(a_lo, b_hi))


def _store_row_slabs(ref, value, rows):
    for j in range(SUBLANES):
        ref[pl.ds(j, rows, stride=SUBLANES), :] = value[:, j * LANES:(j + 1) * LANES]


def _load_row_slabs(ref, rows):
    return [ref[pl.ds(j, rows, stride=SUBLANES), :] for j in range(SUBLANES)]


def _sigmoid(x):
    return 1.0 / (1.0 + jnp.exp(-x))


def _silu(x):
    return x * _sigmoid(x)


def _in_proj_kernel(x_ref, nw_ref, w_ref, qkva_ref, z_ref, ba_ref, qb_ref, kb_ref, vb_ref):
    x = x_ref[...]
    h = x * lax.rsqrt(jnp.mean(x * x, axis=-1, keepdims=True) + NORM_EPS) * nw_ref[...]
    hb = h.astype(BF16)
    c0 = 3 * DN_WIDTH
    c1 = c0 + DN_WIDTH
    c2 = c1 + CA_WIDTH
    c3 = c2 + CA_WIDTH
    c4 = c3 + CA_WIDTH
    qkva_ref[...] = _dot(hb, w_ref[:, 0:c0])
    z_ref[...] = _dot(hb, w_ref[:, c0:c1])
    qb_ref[...] = _dot(hb, w_ref[:, c1:c2]).astype(BF16)
    kb_ref[...] = _dot(hb, w_ref[:, c2:c3]).astype(BF16)
    vb_ref[...] = _dot(hb, w_ref[:, c3:c4]).astype(BF16)
    ba_ref[...] = _dot(hb, w_ref[:, c4:c4 + LANES])


def _in_proj(x2, norm_w, w_all, tm):
    t, d = x2.shape
    wcols = w_all.shape[1]
    row = lambda i: (i, 0)
    fixed = lambda i: (0, 0)
    return pl.pallas_call(
        _in_proj_kernel,
        grid=(t // tm,),
        in_specs=[
            pl.BlockSpec((tm, d), row),
            pl.BlockSpec((1, d), fixed),
            pl.BlockSpec((d, wcols), fixed),
        ],
        out_specs=[
            pl.BlockSpec((tm, 3 * DN_WIDTH), row),
            pl.BlockSpec((tm, DN_WIDTH), row),
            pl.BlockSpec((tm, LANES), row),
            pl.BlockSpec((tm, CA_WIDTH), row),
            pl.BlockSpec((tm, CA_WIDTH), row),
            pl.BlockSpec((tm, CA_WIDTH), row),
        ],
        out_shape=[
            jax.ShapeDtypeStruct((t, 3 * DN_WIDTH), F32),
            jax.ShapeDtypeStruct((t, DN_WIDTH), F32),
            jax.ShapeDtypeStruct((t, LANES), F32),
            jax.ShapeDtypeStruct((t, CA_WIDTH), BF16),
            jax.ShapeDtypeStruct((t, CA_WIDTH), BF16),
            jax.ShapeDtypeStruct((t, CA_WIDTH), BF16),
        ],
        compiler_params=pltpu.CompilerParams(
            dimension_semantics=("parallel",), vmem_limit_bytes=VMEM_LIMIT),
        name="in_proj",
    )(x2, norm_w, w_all)


def _unit_lower_inverses(lowers, eye, inv_block):
    eye16 = eye.astype(BF16)
    bf = lambda a, b: _dot(a, b).astype(BF16)
    each = lambda fn, *lists: [fn(*args) for args in zip(*lists)]
    diag = each(lambda l: jnp.where(inv_block, l, 0.0), lowers)
    diag16 = each(lambda x: x.astype(BF16), diag)
    off16 = each(lambda l, x: (l - x).astype(BF16), lowers, diag)
    d2 = each(lambda x: bf(x, x), diag16)
    d4 = each(lambda x: bf(x, x), d2)
    d8 = each(lambda x: bf(x, x), d4)
    pa = each(lambda x, y: bf(eye16 - x, eye16 + y), diag16, d2)
    pb = each(lambda x, y: bf(eye16 + x, eye16 + y), d4, d8)
    p = each(bf, pa, pb)
    m = each(bf, p, off16)
    m2 = each(lambda x: bf(x, x), m)
    mm = each(lambda x, y: bf(eye16 - x, eye16 + y), m, m2)
    t0 = each(_dot, mm, p)
    t0_hi = each(lambda x: x.astype(BF16), t0)
    resid = each(lambda l, x, x_hi: (eye - x) - _dot_split(l, x, x_hi), lowers, t0, t0_hi)
    return each(lambda x, x_hi, res: x + _dot(x_hi, res.astype(BF16)), t0, t0_hi, resid)


def _dn_prep_kernel(cur_ref, halo_ref, ba_ref, convw_ref, alog_ref, dtb_ref,
                    w_ref, qd_ref, kd_ref, u_ref, qk_ref, cd_ref):
    n = pl.program_id(1)
    r = PREP_ROWS
    cur = cur_ref[0]
    halo = jnp.where(n > 0, halo_ref[0], 0.0)
    full = jnp.concatenate([halo, cur], axis=0)
    conv = full[SUBLANES:] * convw_ref[CONV_WIDTH - 1:CONV_WIDTH, :]
    for j in range(CONV_WIDTH - 1):
        shift = CONV_WIDTH - 1 - j
        conv = conv + pltpu.roll(full, shift, axis=0)[SUBLANES:] * convw_ref[j:j + 1, :]
    qkv = _silu(conv)

    ba = ba_ref[0]
    beta_all = _sigmoid(ba)
    sp_in = ba + dtb_ref[...]
    softplus = jnp.maximum(sp_in, 0.0) + jnp.log1p(jnp.exp(-jnp.abs(sp_in)))
    g_all = -jnp.exp(alog_ref[...]) * softplus
    rows = lax.broadcasted_iota(jnp.int32, (r, r), 0)
    cols = lax.broadcasted_iota(jnp.int32, (r, r), 1)
    same_chunk = (rows // CHUNK) == (cols // CHUNK)
    causal = same_chunk & (rows >= cols)
    strict = same_chunk & (rows > cols)
    inv_block = (rows // INV_BLOCK) == (cols // INV_BLOCK)
    eye = (rows == cols).astype(F32)
    gc_all = _dot(causal.astype(F32), g_all, precision=HIGHEST)
    gc_last_all = _dot(same_chunk.astype(F32), g_all, precision=HIGHEST)
    gc_rows = gc_all.T
    chunk_decay = jnp.exp(gc_last_all)
    cd_ref[0] = jnp.concatenate(
        [chunk_decay[c * CHUNK:c * CHUNK + SUBLANES] for c in range(r // CHUNK)], axis=0)

    scale = DN_HEAD_DIM ** -0.5
    lowers, rhss, qk_parts = [], [], []
    for h in range(DN_HEADS):
        lo = h * DN_HEAD_DIM
        q = qkv[:, lo:lo + DN_HEAD_DIM]
        k = qkv[:, DN_WIDTH + lo:DN_WIDTH + lo + DN_HEAD_DIM]
        v = qkv[:, 2 * DN_WIDTH + lo:2 * DN_WIDTH + lo + DN_HEAD_DIM]
        q = q * lax.rsqrt(jnp.sum(q * q, axis=-1, keepdims=True) + NORM_EPS) * scale
        k = k * lax.rsqrt(jnp.sum(k * k, axis=-1, keepdims=True) + NORM_EPS)
        beta = beta_all[:, h:h + 1]
        gc = gc_all[:, DN_HEADS + h:DN_HEADS + h + 1]
        gc_row = gc_rows[DN_HEADS + h:DN_HEADS + h + 1, :]
        gc_last = gc_last_all[:, DN_HEADS + h:DN_HEADS + h + 1]
        decay = jnp.where(causal, jnp.exp(jnp.where(causal, gc - gc_row, 0.0)), 0.0)
        egc = jnp.exp(gc)
        k_beta = k * beta
        k16 = k.astype(BF16)
        lowers.append(jnp.where(strict, _dot_nt(k_beta.astype(BF16), k16) * decay, 0.0))
        rhss.append(jnp.concatenate([v * beta, k_beta * egc], axis=-1).astype(BF16))
        qd_ref[0, :, lo:lo + DN_HEAD_DIM] = (q * egc).astype(BF16)
        kd_ref[0, :, lo:lo + DN_HEAD_DIM] = k * jnp.exp(gc_last - gc)
        qk = _dot_nt(q.astype(BF16), k16) * decay
        compact = qk[:, 0:CHUNK]
        for c in range(1, r // CHUNK):
            compact = compact + qk[:, c * CHUNK:(c + 1) * CHUNK]
        qk_parts.append(compact.astype(BF16))
    qk_ref[0] = jnp.concatenate(qk_parts, axis=-1)

    t_invs = _unit_lower_inverses(lowers, eye, inv_block)
    for h in range(DN_HEADS):
        lo = h * DN_HEAD_DIM
        sol = _dot(t_invs[h].astype(BF16), rhss[h])
        u_ref[0, :, lo:lo + DN_HEAD_DIM] = sol[:, :DN_HEAD_DIM]
        w_ref[0, :, lo:lo + DN_HEAD_DIM] = sol[:, DN_HEAD_DIM:].astype(BF16)


def _dn_prep(qkva, ba, conv_w, alog_row, dtb_row):
    b, s, _ = qkva.shape
    r = PREP_ROWS
    n_chunks = s // CHUNK
    halo_blocks = r // SUBLANES
    fixed = lambda i, n: (0, 0)
    blk = lambda i, n: (i, n, 0)
    return pl.pallas_call(
        _dn_prep_kernel,
        grid=(b, s // r),
        in_specs=[
            pl.BlockSpec((1, r, 3 * DN_WIDTH), blk),
            pl.BlockSpec((1, SUBLANES, 3 * DN_WIDTH),
                         lambda i, n: (i, jnp.maximum(n * halo_blocks - 1, 0), 0)),
            pl.BlockSpec((1, r, LANES), blk),
            pl.BlockSpec((CONV_WIDTH, 3 * DN_WIDTH), fixed),
            pl.BlockSpec((1, LANES), fixed),
            pl.BlockSpec((1, LANES), fixed),
        ],
        out_specs=[
            pl.BlockSpec((1, r, DN_WIDTH), blk),
            pl.BlockSpec((1, r, DN_WIDTH), blk),
            pl.BlockSpec((1, r, DN_WIDTH), blk),
            pl.BlockSpec((1, r, DN_WIDTH), blk),
            pl.BlockSpec((1, r, DN_HEADS * CHUNK), blk),
            pl.BlockSpec((1, r // CHUNK * SUBLANES, LANES), blk),
        ],
        out_shape=[
            jax.ShapeDtypeStruct((b, s, DN_WIDTH), BF16),
            jax.ShapeDtypeStruct((b, s, DN_WIDTH), BF16),
            jax.ShapeDtypeStruct((b, s, DN_WIDTH), F32),
            jax.ShapeDtypeStruct((b, s, DN_WIDTH), F32),
            jax.ShapeDtypeStruct((b, s, DN_HEADS * CHUNK), BF16),
            jax.ShapeDtypeStruct((b, n_chunks * SUBLANES, LANES), F32),
        ],
        compiler_params=pltpu.CompilerParams(
            dimension_semantics=("parallel", "parallel"), vmem_limit_bytes=VMEM_LIMIT),
        name="dn_prep",
    )(qkva, qkva, ba, conv_w, alog_row, dtb_row)


def _dn_scan_kernel(w_ref, qd_ref, kd_ref, u_ref, qk_ref, cd_ref, z_ref, normw_ref, o_ref, state_ref):
    n = pl.program_id(0)

    @pl.when(n == 0)
    def _():
        state_ref[...] = jnp.zeros_like(state_ref)

    n_batch = w_ref.shape[0]
    chains = [(bi, h) for bi in range(n_batch) for h in range(DN_HEADS)]
    for g in range(0, len(chains), SCAN_GROUP):
        group = chains[g:g + SCAN_GROUP]
        col = lambda h: slice(h * DN_HEAD_DIM, (h + 1) * DN_HEAD_DIM)
        s16 = [state_ref[bi * DN_HEADS + h].astype(BF16) for bi, h in group]
        ws = [_dot(w_ref[bi, :, col(h)], s) for (bi, h), s in zip(group, s16)]
        qs = [_dot(qd_ref[bi, :, col(h)], s) for (bi, h), s in zip(group, s16)]
        vn16 = [(u_ref[bi, :, col(h)] - x).astype(BF16) for (bi, h), x in zip(group, ws)]
        inner = [_dot(qk_ref[bi, :, h * CHUNK:(h + 1) * CHUNK], v) for (bi, h), v in zip(group, vn16)]
        kd_t = [kd_ref[bi, :, col(h)].T.astype(BF16) for bi, h in group]
        upd = [_dot(k, v) for k, v in zip(kd_t, vn16)]
        for (bi, h), x in zip(group, upd):
            cd = cd_ref[bi, 0:1, DN_HEADS + h:DN_HEADS + h + 1]
            state_ref[bi * DN_HEADS + h] = state_ref[bi * DN_HEADS + h] * cd + x
        for (bi, h), a, c in zip(group, qs, inner):
            o = a + c
            o = o * lax.rsqrt(jnp.mean(o * o, axis=-1, keepdims=True) + NORM_EPS) * normw_ref[...]
            o_ref[bi, :, col(h)] = o * _silu(z_ref[bi, :, col(h)])


def _dn_scan(w, qd, kd, u, qk, cd, z, norm_w):
    b, s, _ = w.shape
    n_chunks = s // CHUNK
    blk = lambda n: (0, n, 0)
    wide = pl.BlockSpec((b, CHUNK, DN_WIDTH), blk)
    return pl.pallas_call(
        _dn_scan_kernel,
        grid=(n_chunks,),
        in_specs=[
            wide, wide, wide, wide,
            pl.BlockSpec((b, CHUNK, DN_HEADS * CHUNK), blk),
            pl.BlockSpec((b, SUBLANES, LANES), blk),
            wide,
            pl.BlockSpec((1, DN_HEAD_DIM), lambda n: (0, 0)),
        ],
        out_specs=wide,
        out_shape=jax.ShapeDtypeStruct((b, s, DN_WIDTH), F32),
        scratch_shapes=[pltpu.VMEM((b * DN_HEADS, DN_HEAD_DIM, DN_HEAD_DIM), F32)],
        compiler_params=pltpu.CompilerParams(
            dimension_semantics=("arbitrary",), vmem_limit_bytes=VMEM_LIMIT),
        name="dn_scan",
    )(w, qd, kd, u, qk, cd, z, norm_w)


def _band_attn_kernel(q_ref, k_ref, v_ref, ext_ref, nw_ref, o_ref, bias_ref):
    first = jnp.logical_and(pl.program_id(0) == 0, pl.program_id(1) == 0)

    @pl.when(first)
    def _():
        qi = lax.broadcasted_iota(jnp.int32, (ATT_ROWS, ATT_WIN), 0) // CHUNK
        kj = lax.broadcasted_iota(jnp.int32, (ATT_ROWS, ATT_WIN), 1) // CHUNK
        in_band = jnp.logical_and(kj >= qi, kj <= qi + CA_LEFT_CHUNKS)
        for h in range(CA_HEADS):
            tiled = jnp.broadcast_to(ext_ref[h:h + 1, :], (ATT_ROWS, ATT_EXT))
            toeplitz = pltpu.roll(tiled, 0, axis=1, stride=1, stride_axis=0)
            bias_ref[h] = jnp.where(in_band, toeplitz[:, :ATT_WIN], -jnp.inf)

    n0 = pl.program_id(1) * (ATT_ROWS // CHUNK)
    k_parts, v_parts = [], []
    for w in range(ATT_WIN // CHUNK):
        start = pl.multiple_of(jnp.maximum(n0 - CA_LEFT_CHUNKS + w, 0) * CHUNK, CHUNK)
        k_parts.append(k_ref[0, pl.ds(start, CHUNK), :])
        v_parts.append(v_ref[0, pl.ds(start, CHUNK), :])
    k_win = jnp.concatenate(k_parts, axis=0)
    v_win = jnp.concatenate(v_parts, axis=0)
    q = q_ref[0] * jnp.asarray(CA_HEAD_DIM ** -0.5, BF16)

    key_col = lax.broadcasted_iota(jnp.int32, (1, ATT_WIN), 1)
    before_start = key_col < (CA_LEFT_CHUNKS - n0) * CHUNK
    col_bias = jnp.where(before_start, -jnp.inf, 0.0)
    lane = lax.broadcasted_iota(jnp.int32, (ATT_ROWS, LANES), 1)
    heads_per_group = LANES // CA_HEAD_DIM

    outs = []
    for grp in range(CA_WIDTH // LANES):
        lo = grp * LANES
        qg = q[:, lo:lo + LANES]
        kg = k_win[:, lo:lo + LANES]
        vg = v_win[:, lo:lo + LANES]
        og = jnp.zeros((ATT_ROWS, LANES), F32)
        for hh in range(heads_per_group):
            head = grp * heads_per_group + hh
            in_head = (lane // CA_HEAD_DIM) == hh
            qm = jnp.where(in_head, qg, jnp.zeros_like(qg))
            s = _dot_nt(qm, kg) + bias_ref[head] + col_bias
            e = jnp.exp(s - jnp.max(s, axis=-1, keepdims=True))
            denom = jnp.sum(e, axis=-1, keepdims=True)
            og = jnp.where(in_head, _dot(e.astype(BF16), vg) / denom, og)
        outs.append(og)
    o = jnp.concatenate(outs, axis=-1)
    o = o * lax.rsqrt(jnp.mean(o * o, axis=-1, keepdims=True) + NORM_EPS) * nw_ref[...]
    o_ref[0] = o


def _band_attn(qb, kb, vb, ext, norm_w):
    b, s, _ = qb.shape
    return pl.pallas_call(
        _band_attn_kernel,
        grid=(b, s // ATT_ROWS),
        in_specs=[
            pl.BlockSpec((1, ATT_ROWS, CA_WIDTH), lambda i, n: (i, n, 0)),
            pl.BlockSpec((1, s, CA_WIDTH), lambda i, n: (i, 0, 0)),
            pl.BlockSpec((1, s, CA_WIDTH), lambda i, n: (i, 0, 0)),
            pl.BlockSpec((CA_HEADS, ATT_EXT), lambda i, n: (0, 0)),
            pl.BlockSpec((1, CA_WIDTH), lambda i, n: (0, 0)),
        ],
        out_specs=pl.BlockSpec((1, ATT_ROWS, CA_WIDTH), lambda i, n: (i, n, 0)),
        out_shape=jax.ShapeDtypeStruct((b, s, CA_WIDTH), F32),
        scratch_shapes=[pltpu.VMEM((CA_HEADS, ATT_ROWS, ATT_WIN), F32)],
        compiler_params=pltpu.CompilerParams(
            dimension_semantics=("arbitrary", "arbitrary"), vmem_limit_bytes=VMEM_LIMIT),
        name="band_attn",
    )(qb, kb, vb, ext, norm_w)


def _out_router_kernel(oa_ref, ob_ref, x_ref, wout_ref, n2_ref, wr_hi_ref, wr_lo_ref, br_ref,
                       x1_ref, h2_ref, route_ref, gates_ref, counts_ref, run_ref, earlier_ref):
    i = pl.program_id(0)
    tm = x_ref.shape[0]

    @pl.when(i == 0)
    def _():
        run_ref[...] = jnp.zeros_like(run_ref)
        r = lax.broadcasted_iota(jnp.int32, (tm, tm), 0)
        c = lax.broadcasted_iota(jnp.int32, (tm, tm), 1)
        earlier_ref[...] = (r > c).astype(BF16)

    x1 = (x_ref[...]
          + _dot(oa_ref[...].astype(BF16), wout_ref[0:DN_WIDTH, :])
          + _dot(ob_ref[...].astype(BF16), wout_ref[DN_WIDTH:, :]))
    x1_ref[...] = x1
    h2 = x1 * lax.rsqrt(jnp.mean(x1 * x1, axis=-1, keepdims=True) + NORM_EPS) * n2_ref[...]
    _store_row_slabs(h2_ref, h2, tm)
    lane = lax.broadcasted_iota(jnp.int32, (tm, LANES), 1)
    h2_hi = h2.astype(BF16)
    h2_lo = (h2 - h2_hi.astype(F32)).astype(BF16)
    logits = (_dot(h2_hi, wr_hi_ref[...]) + (_dot(h2_hi, wr_lo_ref[...]) + _dot(h2_lo, wr_hi_ref[...]))
              + br_ref[...])
    logits = jnp.where(lane < N_EXPERTS, logits, -jnp.inf)

    top_vals, top_idx, onehots = [], [], []
    for _ in range(TOP_K):
        m = jnp.max(logits, axis=-1, keepdims=True)
        idx = jnp.min(jnp.where(logits == m, lane, LANES), axis=-1, keepdims=True)
        hit = lane == idx
        top_vals.append(m)
        top_idx.append(idx)
        onehots.append(hit)
        logits = jnp.where(hit, -jnp.inf, logits)

    exps = [jnp.exp(v - top_vals[0]) for v in top_vals]
    denom = exps[0] + exps[1] + exps[2] + exps[3]
    gates = jnp.zeros((tm, LANES), F32)
    for k in range(TOP_K):
        gates = jnp.where(lane == k, exps[k] / denom, gates)
    gates_ref[...] = gates

    multi = jnp.zeros((tm, LANES), F32)
    for hit in onehots:
        multi = multi + hit.astype(F32)
    before = _dot(earlier_ref[...], multi.astype(BF16)) + run_ref[...]
    route = jnp.zeros((tm, LANES), jnp.int32)
    for k in range(TOP_K):
        rank = jnp.sum(jnp.where(onehots[k], before, 0.0), axis=-1, keepdims=True).astype(jnp.int32)
        route = jnp.where(lane == k, top_idx[k], route)
        route = jnp.where(lane == TOP_K + k, rank, route)
    route_ref[...] = route
    run_ref[...] = run_ref[...] + jnp.sum(multi, axis=0, keepdims=True)
    counts_ref[...] = run_ref[...]


def _out_router(oa, ob, x2, w_out, norm2_w, wr_hi, wr_lo, br_pad, tm):
    t, d = x2.shape
    row = lambda i: (i, 0)
    fixed = lambda i: (0, 0)
    return pl.pallas_call(
        _out_router_kernel,
        grid=(t // tm,),
        in_specs=[
            pl.BlockSpec((tm, DN_WIDTH), row),
            pl.BlockSpec((tm, CA_WIDTH), row),
            pl.BlockSpec((tm, d), row),
            pl.BlockSpec((DN_WIDTH + CA_WIDTH, d), fixed),
            pl.BlockSpec((1, d), fixed),
            pl.BlockSpec((d, LANES), fixed),
            pl.BlockSpec((d, LANES), fixed),
            pl.BlockSpec((1, LANES), fixed),
        ],
        out_specs=[
            pl.BlockSpec((tm, d), row),
            pl.BlockSpec((tm * SUBLANES, LANES), row),
            pl.BlockSpec((tm, LANES), row),
            pl.BlockSpec((tm, LANES), row),
            pl.BlockSpec((1, LANES), fixed),
        ],
        out_shape=[
            jax.ShapeDtypeStruct((t, d), F32),
            jax.ShapeDtypeStruct((t * SUBLANES, LANES), F32),
            jax.ShapeDtypeStruct((t, LANES), jnp.int32),
            jax.ShapeDtypeStruct((t, LANES), F32),
            jax.ShapeDtypeStruct((1, LANES), F32),
        ],
        scratch_shapes=[pltpu.VMEM((1, LANES), F32), pltpu.VMEM((tm, tm), BF16)],
        compiler_params=pltpu.CompilerParams(
            dimension_semantics=("arbitrary",), vmem_limit_bytes=VMEM_LIMIT),
        name="out_router",
    )(oa, ob, x2, w_out, norm2_w, wr_hi, wr_lo, br_pad)


def _invert_slots_kernel(pos_ref, fill_ref, o_ref, sem):
    i = pl.program_id(0)
    chunk = pos_ref.shape[2]

    @pl.when(i == 0)
    def _():
        fill = pltpu.make_async_copy(fill_ref, o_ref, sem)
        fill.start()
        fill.wait()

    base = i * chunk

    def place(a, carry):
        o_ref[pos_ref[0, 0, a]] = base + a
        return carry

    lax.fori_loop(0, chunk, place, 0, unroll=INVERT_UNROLL)


def _invert_slots(pos, n_slots):
    n = pos.shape[0]
    chunk = INVERT_CHUNK
    return pl.pallas_call(
        _invert_slots_kernel,
        grid=(n // chunk,),
        in_specs=[pl.BlockSpec((1, 1, chunk), lambda i: (i, 0, 0), memory_space=pltpu.SMEM),
                  pl.BlockSpec(memory_space=pl.ANY)],
        out_specs=pl.BlockSpec((n_slots,), lambda i: (0,), memory_space=pltpu.SMEM),
        out_shape=jax.ShapeDtypeStruct((n_slots,), jnp.int32),
        scratch_shapes=[pltpu.SemaphoreType.DMA],
        compiler_params=pltpu.CompilerParams(dimension_semantics=("arbitrary",)),
        name="invert_slots",
    )(pos.reshape(n // chunk, 1, chunk), jnp.full((n_slots,), -1, jnp.int32))


def _experts_kernel(be_ref, *refs):
    refs = list(refs)
    take = lambda n: [refs.pop(0) for _ in range(n)]
    src_first_refs = take(RING - 1)
    src_ahead_ref, dst_prev_ref, dst_cur_ref = take(3)
    dst_spare_refs = take(RING - 2)
    h2_ref, wg_ref, bg_ref, wu_ref, bu_ref, wd_ref, bd_ref, g_ref = take(8)
    xbufs = take(RING)
    ybufs = take(RING)
    wg16, wu16, wd16, sem_in, sem_out = refs
    b = pl.program_id(0)
    last = pl.num_programs(0) - 1
    slab = SUBLANES
    bm = xbufs[0].shape[0] // slab

    def gather_rows(idx_ref, slot):
        for r in range(bm):
            src = pl.multiple_of(idx_ref[0, 0, r], slab)
            pltpu.make_async_copy(h2_ref.at[pl.ds(src, slab)],
                                  xbufs[slot].at[pl.ds(r * slab, slab)], sem_in.at[slot]).start()

    def scatter_rows(idx_ref, slot):
        for r in range(bm):
            dst = pl.multiple_of(idx_ref[0, 0, r], slab)
            pltpu.make_async_copy(ybufs[slot].at[pl.ds(r * slab, slab)],
                                  g_ref.at[pl.ds(dst, slab)], sem_out.at[slot]).start()

    def wait_gather(slot):
        pltpu.make_async_copy(h2_ref.at[pl.ds(0, bm * slab)], xbufs[slot], sem_in.at[slot]).wait()

    def wait_scatter(slot):
        pltpu.make_async_copy(ybufs[slot], g_ref.at[pl.ds(0, bm * slab)], sem_out.at[slot]).wait()

    @pl.when(b == 0)
    def _():
        for slot in range(RING - 1):
            gather_rows(src_first_refs[slot], slot)
        for slot in range(1, RING):
            ybufs[slot][...] = jnp.zeros(ybufs[slot].shape, F32)
        for slot in range(1, RING - 1):
            scatter_rows(dst_spare_refs[slot - 1], slot)

    changed = jnp.logical_or(b == 0, be_ref[b] != be_ref[jnp.maximum(b - 1, 0)])

    @pl.when(changed)
    def _():
        wg16[...] = wg_ref[0].astype(BF16)
        wu16[...] = wu_ref[0].astype(BF16)
        wd16[...] = wd_ref[0].astype(BF16)

    def step(cur):
        prv = (cur + RING - 1) % RING
        nxt = (cur + 1) % RING
        wait_gather(cur)
        gather_rows(src_ahead_ref, prv)
        scatter_rows(dst_prev_ref, prv)
        x = jnp.concatenate(_load_row_slabs(xbufs[cur], bm), axis=-1).astype(BF16)
        gate = _dot(x, wg16[...]) + bg_ref[0]
        up = _dot(x, wu16[...]) + bu_ref[0]
        gate = jnp.minimum(gate, SWIGLU_LIMIT)
        up = jnp.clip(up, -SWIGLU_LIMIT, SWIGLU_LIMIT)
        glu = gate * _sigmoid(gate * SWIGLU_ALPHA)
        mid = ((up + 1.0) * glu).astype(BF16)
        _store_row_slabs(ybufs[cur], _dot(mid, wd16[...]) + bd_ref[0], bm)
        wait_scatter(nxt)

        @pl.when(b == last)
        def _():
            scatter_rows(dst_cur_ref, cur)
            for back in range(RING - 1):
                wait_scatter((cur + RING - back) % RING)
            for ahead in range(1, RING):
                wait_gather((cur + ahead) % RING)

    for slot in range(RING):
        pl.when(b % RING == slot)(functools.partial(step, slot))


def _experts(h2, slot_src, slot_dst, block_e, n_out_rows, wg, bg, wu, bu, wd, bd, bm):
    slab = SUBLANES
    d = wg.shape[1]
    f = wg.shape[2]
    n_blocks = slot_src.shape[0] // bm
    spare = lambda blk: n_out_rows - (RING - blk % RING) * bm + jnp.arange(bm, dtype=jnp.int32)
    src3 = (slot_src * slab).reshape(n_blocks, 1, bm)
    standins = [spare(j - RING) for j in range(1, RING - 1)]
    dst3 = (jnp.concatenate([spare(-1), slot_dst] + standins) * slab).reshape(n_blocks + RING - 1, 1, bm)
    smem_blk = lambda imap: pl.BlockSpec((1, 1, bm), imap, memory_space=pltpu.SMEM)
    fixed_blk = lambda j: smem_blk(lambda b, be: (j, 0, 0))
    wmap = lambda b, be: (be[b], 0, 0)
    grid_spec = pltpu.PrefetchScalarGridSpec(
        num_scalar_prefetch=1,
        grid=(n_blocks,),
        in_specs=[fixed_blk(j) for j in range(RING - 1)] + [
            smem_blk(lambda b, be: (jnp.minimum(b + RING - 1, n_blocks - 1), 0, 0)),
            smem_blk(lambda b, be: (b, 0, 0)),
            smem_blk(lambda b, be: (b + 1, 0, 0)),
        ] + [fixed_blk(n_blocks + j) for j in range(1, RING - 1)] + [
            pl.BlockSpec(memory_space=pl.ANY),
            pl.BlockSpec((1, d, f), wmap),
            pl.BlockSpec((1, 1, f), wmap),
            pl.BlockSpec((1, d, f), wmap),
            pl.BlockSpec((1, 1, f), wmap),
            pl.BlockSpec((1, f, d), wmap),
            pl.BlockSpec((1, 1, d), wmap),
        ],
        out_specs=pl.BlockSpec(memory_space=pl.ANY),
        scratch_shapes=[pltpu.VMEM((bm * slab, LANES), F32)] * (2 * RING) + [
            pltpu.VMEM((d, f), BF16),
            pltpu.VMEM((d, f), BF16),
            pltpu.VMEM((f, d), BF16),
            pltpu.SemaphoreType.DMA((RING,)),
            pltpu.SemaphoreType.DMA((RING,)),
        ],
    )
    return pl.pallas_call(
        _experts_kernel,
        grid_spec=grid_spec,
        out_shape=jax.ShapeDtypeStruct((n_out_rows * slab, LANES), F32),
        compiler_params=pltpu.CompilerParams(
            dimension_semantics=("arbitrary",), vmem_limit_bytes=VMEM_LIMIT),
        name="experts",
    )(block_e, *([src3] * RING), *([dst3] * RING), h2, wg, bg, wu, bu, wd, bd)


def _combine_kernel(y0_ref, y1_ref, y2_ref, y3_ref, x1_ref, gates_ref, fw_ref, o_ref):
    tm = x1_ref.shape[0]
    acc = x1_ref[...]
    gates = gates_ref[...]
    for k, y_ref in enumerate((y0_ref, y1_ref, y2_ref, y3_ref)):
        y = jnp.concatenate(_load_row_slabs(y_ref, tm), axis=-1)
        acc = acc + y * gates[:, k:k + 1]
    o_ref[...] = acc * lax.rsqrt(jnp.mean(acc * acc, axis=-1, keepdims=True) + NORM_EPS) * fw_ref[...]


def _combine(ys, x1, gates, final_w, tm):
    t, d = x1.shape
    n_tiles = t // tm
    row = lambda i: (i, 0)
    y_spec = lambda k: pl.BlockSpec((tm * SUBLANES, LANES), lambda i: (k * n_tiles + i, 0))
    return pl.pallas_call(
        _combine_kernel,
        grid=(n_tiles,),
        in_specs=[y_spec(k) for k in range(TOP_K)] + [
            pl.BlockSpec((tm, d), row),
            pl.BlockSpec((tm, LANES), row),
            pl.BlockSpec((1, d), lambda i: (0, 0)),
        ],
        out_specs=pl.BlockSpec((tm, d), row),
        out_shape=jax.ShapeDtypeStruct((t, d), F32),
        compiler_params=pltpu.CompilerParams(
            dimension_semantics=("parallel",), vmem_limit_bytes=VMEM_LIMIT),
        name="combine",
    )(ys, ys, ys, ys, x1, gates, final_w)


def _rel_bias_rows(rel_bias):
    m = np.arange(ATT_EXT)
    m = np.where(m < ATT_WIN, m, m - ATT_EXT)
    dist = CA_LEFT_CHUNKS * CHUNK - m
    idx = np.clip(dist, -MAX_REL_DIST, MAX_REL_DIST) + MAX_REL_DIST
    return rel_bias.astype(F32)[:, idx]


def _layer(x, norm1_w, w_in, conv_w, a_log, dt_bias, dn_norm_w, rel_bias, attn_norm_w, w_out,
           norm2_w, w_router, b_router, w_gate, b_gate, w_up, b_up, w_down, b_down, out_norm_w):
    b, s, d = x.shape
    assert d == SUBLANES * LANES, "row-slab layout holds one model row per (8, 128) tile"
    assert s % ATT_ROWS == 0 and s % PREP_ROWS == 0
    t = b * s
    x2 = x.reshape(t, d)

    ba_lo = 4 * DN_WIDTH
    ba_hi = ba_lo + 2 * DN_HEADS
    w_small = jnp.pad(w_in[:, ba_lo:ba_hi], ((0, 0), (0, LANES - 2 * DN_HEADS)))
    w_all = jnp.concatenate([w_in[:, :ba_lo], w_in[:, ba_hi:], w_small], axis=1).astype(BF16)
    qkva, z_a, ba, q_b, k_b, v_b = _in_proj(x2, norm1_w.reshape(1, d), w_all, tm=512)

    lane_pad = (DN_HEADS, LANES - 2 * DN_HEADS)
    alog_row = jnp.pad(a_log.astype(F32), lane_pad).reshape(1, LANES)
    dtb_row = jnp.pad(dt_bias.astype(F32), lane_pad).reshape(1, LANES)
    dn_w, dn_qd, dn_kd, dn_u, dn_qk, dn_cd = _dn_prep(
        qkva.reshape(b, s, 3 * DN_WIDTH), ba.reshape(b, s, LANES), conv_w, alog_row, dtb_row)
    o_a = _dn_scan(dn_w, dn_qd, dn_kd, dn_u, dn_qk, dn_cd, z_a.reshape(b, s, DN_WIDTH),
                   dn_norm_w.reshape(1, DN_HEAD_DIM))
    o_b = _band_attn(q_b.reshape(b, s, CA_WIDTH), k_b.reshape(b, s, CA_WIDTH),
                     v_b.reshape(b, s, CA_WIDTH), _rel_bias_rows(rel_bias),
                     attn_norm_w.reshape(1, CA_WIDTH))

    wr_pad = jnp.pad(w_router.astype(F32), ((0, 0), (0, LANES - N_EXPERTS)))
    wr_hi = wr_pad.astype(BF16)
    br_pad = jnp.pad(b_router.astype(F32), (0, LANES - N_EXPERTS)).reshape(1, LANES)
    x1, h2, route, gates, counts = _out_router(
        o_a.reshape(t, DN_WIDTH), o_b.reshape(t, CA_WIDTH), x2, w_out.astype(BF16),
        norm2_w.reshape(1, d), wr_hi, (wr_pad - wr_hi.astype(F32)).astype(BF16), br_pad, tm=512)

    bm = EXPERT_BLOCK_ROWS
    n_assign = t * TOP_K
    n_blocks = n_assign // bm + N_EXPERTS
    n_slots = n_blocks * bm
    counts_i = counts[0, :N_EXPERTS].astype(jnp.int32)
    padded = (counts_i + bm - 1) // bm * bm
    pad_end = jnp.cumsum(padded)
    pad_start = pad_end - padded
    top_e = route[:, :TOP_K]
    expert_ids = jnp.arange(N_EXPERTS, dtype=jnp.int32)
    start_of = jnp.sum(jnp.where(top_e[..., None] == expert_ids, pad_start, 0), axis=-1)
    pos = start_of + route[:, TOP_K:2 * TOP_K]
    slot_assign = _invert_slots(pos.reshape(-1), n_slots)
    n_out_rows = n_assign + RING * bm
    slot_id = jnp.arange(n_slots, dtype=jnp.int32)
    spare_row = n_assign + (slot_id // bm % RING) * bm + slot_id % bm
    is_real = slot_assign >= 0
    slot_src = jnp.where(is_real, slot_assign // TOP_K, 0)
    slot_dst = jnp.where(is_real, (slot_assign % TOP_K) * t + slot_assign // TOP_K, spare_row)
    block_start = jnp.arange(n_blocks, dtype=jnp.int32) * bm
    block_e = jnp.minimum(jnp.sum((pad_end[None, :] <= block_start[:, None]).astype(jnp.int32), axis=1),
                          N_EXPERTS - 1)

    f = w_gate.shape[-1]
    ys = _experts(h2, slot_src, slot_dst, block_e, n_out_rows,
                  w_gate, b_gate.reshape(N_EXPERTS, 1, f), w_up, b_up.reshape(N_EXPERTS, 1, f),
                  w_down, b_down.reshape(N_EXPERTS, 1, d), bm)
    out = _combine(ys, x1, gates, out_norm_w.reshape(1, d), tm=256)
    return out.reshape(b, s, d)


def kernel(x, norm1_w, w_in, conv_w, a_log, dt_bias, dn_norm_w, rel_bias, attn_norm_w, w_out, norm2_w, w_router, b_router, w_gate, b_gate, w_up, b_up, w_down, b_down, final_norm_w):
    depth = norm1_w.shape[0]
    assert depth == 1, "the final RMSNorm is fused into the last layer's combine step"
    return _layer(x, norm1_w[0], w_in[0], conv_w[0], a_log[0], dt_bias[0], dn_norm_w[0],
                  rel_bias[0], attn_norm_w[0], w_out[0], norm2_w[0], w_router[0], b_router[0],
                  w_gate[0], b_gate[0], w_up[0], b_up[0], w_down[0], b_down[0], final_norm_w)
```

```python
import functools

import jax
import numpy as np
import jax.numpy as jnp
from jax import lax
from jax.experimental import pallas as pl
from jax.experimental.pallas import tpu as pltpu

F32 = jnp.float32
BF16 = jnp.bfloat16
HIGHEST = lax.Precision.HIGHEST

NORM_EPS = 1e-6
CHUNK = 64
DN_HEADS = 4
DN_HEAD_DIM = 128
DN_WIDTH = DN_HEADS * DN_HEAD_DIM
CONV_WIDTH = 4
CA_HEADS = 8
CA_HEAD_DIM = 64
CA_WIDTH = CA_HEADS * CA_HEAD_DIM
CA_LEFT_CHUNKS = 8
MAX_REL_DIST = 256
N_EXPERTS = 32
TOP_K = 4
SWIGLU_LIMIT = 7.0
SWIGLU_ALPHA = 1.702

LANES = 128
SUBLANES = 8
INV_BLOCK = 16
PREP_ROWS = 256
SCAN_GROUP = 32
ATT_ROWS = 256
ATT_WIN = ATT_ROWS + CA_LEFT_CHUNKS * CHUNK
ATT_EXT = 1024
EXPERT_BLOCK_ROWS = 256
OUT_SLAB = 4
RING = 4
INVERT_CHUNK = 8192
INVERT_UNROLL = 8
VMEM_LIMIT = 48 * 1024 * 1024


def _dot(a, b, dims=(((1,), (0,)), ((), ())), precision=None):
    return lax.dot_general(a, b, dims, precision=precision, preferred_element_type=F32)


def _dot_nt(a, b, precision=None):
    return _dot(a, b, (((1,), (1,)), ((), ())), precision)


def _bdot(a, b):
    return _dot(a.astype(BF16), b.astype(BF16))


def _dot_split(a, b, b_hi):
    a_hi = a.astype(BF16)
    a_lo = (a - a_hi.astype(F32)).astype(BF16)
    b_lo = (b - b_hi.astype(F32)).astype(BF16)
    return _dot(a_hi, b_hi) + (_dot(a_hi, b_lo) + _dot(a_lo, b_hi))


def _store_row_slabs(ref, value, rows):
    n = value.shape[1] // LANES
    for j in range(n):
        ref[pl.ds(j, rows, stride=n), :] = value[:, j * LANES:(j + 1) * LANES]


def _load_row_slabs(ref, rows):
    n = ref.shape[0] // rows
    return jnp.concatenate([ref[pl.ds(j, rows, stride=n), :] for j in range(n)], axis=-1)


def _pack_bf16_pairs(x):
    c = x.shape[1] // 2
    return pltpu.pack_elementwise([x[:, :c], x[:, c:]], packed_dtype=BF16)


def _unpack_bf16_pairs(w):
    halves = [pltpu.unpack_elementwise(w, index=i, packed_dtype=BF16, unpacked_dtype=F32) for i in range(2)]
    return jnp.concatenate(halves, axis=-1)


def _sigmoid(x):
    return 1.0 / (1.0 + jnp.exp(-x))


def _silu(x):
    return x * _sigmoid(x)


def _in_proj_kernel(x_ref, nw_ref, w_ref, qkva_ref, z_ref, ba_ref, qb_ref, kb_ref, vb_ref):
    x = x_ref[...]
    h = x * lax.rsqrt(jnp.mean(x * x, axis=-1, keepdims=True) + NORM_EPS) * nw_ref[...]
    hb = h.astype(BF16)
    c0 = 3 * DN_WIDTH
    c1 = c0 + DN_WIDTH
    c2 = c1 + CA_WIDTH
    c3 = c2 + CA_WIDTH
    c4 = c3 + CA_WIDTH
    qkva_ref[...] = _dot(hb, w_ref[:, 0:c0])
    z_ref[...] = _dot(hb, w_ref[:, c0:c1])
    qb_ref[...] = _dot(hb, w_ref[:, c1:c2]).astype(BF16)
    kb_ref[...] = _dot(hb, w_ref[:, c2:c3]).astype(BF16)
    vb_ref[...] = _dot(hb, w_ref[:, c3:c4]).astype(BF16)
    ba_ref[...] = _dot(hb, w_ref[:, c4:c4 + LANES])


def _in_proj(x2, norm_w, w_all, tm):
    t, d = x2.shape
    wcols = w_all.shape[1]
    row = lambda i: (i, 0)
    fixed = lambda i: (0, 0)
    return pl.pallas_call(
        _in_proj_kernel,
        grid=(t // tm,),
        in_specs=[
            pl.BlockSpec((tm, d), row),
            pl.BlockSpec((1, d), fixed),
            pl.BlockSpec((d, wcols), fixed),
        ],
        out_specs=[
            pl.BlockSpec((tm, 3 * DN_WIDTH), row),
            pl.BlockSpec((tm, DN_WIDTH), row),
            pl.BlockSpec((tm, LANES), row),
            pl.BlockSpec((tm, CA_WIDTH), row),
            pl.BlockSpec((tm, CA_WIDTH), row),
            pl.BlockSpec((tm, CA_WIDTH), row),
        ],
        out_shape=[
            jax.ShapeDtypeStruct((t, 3 * DN_WIDTH), F32),
            jax.ShapeDtypeStruct((t, DN_WIDTH), F32),
            jax.ShapeDtypeStruct((t, LANES), F32),
            jax.ShapeDtypeStruct((t, CA_WIDTH), BF16),
            jax.ShapeDtypeStruct((t, CA_WIDTH), BF16),
            jax.ShapeDtypeStruct((t, CA_WIDTH), BF16),
        ],
        compiler_params=pltpu.CompilerParams(
            dimension_semantics=("parallel",), vmem_limit_bytes=VMEM_LIMIT),
        name="in_proj",
    )(x2, norm_w, w_all)


def _unit_lower_inverses(lowers, eye, inv_block):
    eye16 = eye.astype(BF16)
    bf = lambda a, b: _dot(a, b).astype(BF16)
    each = lambda fn, *lists: [fn(*args) for args in zip(*lists)]
    diag = each(lambda l: jnp.where(inv_block, l, 0.0), lowers)
    diag16 = each(lambda x: x.astype(BF16), diag)
    off16 = each(lambda l, x: (l - x).astype(BF16), lowers, diag)
    d2 = each(lambda x: bf(x, x), diag16)
    d4 = each(lambda x: bf(x, x), d2)
    d8 = each(lambda x: bf(x, x), d4)
    pa = each(lambda x, y: bf(eye16 - x, eye16 + y), diag16, d2)
    pb = each(lambda x, y: bf(eye16 + x, eye16 + y), d4, d8)
    p = each(bf, pa, pb)
    m = each(bf, p, off16)
    m2 = each(lambda x: bf(x, x), m)
    mm = each(lambda x, y: bf(eye16 - x, eye16 + y), m, m2)
    t0 = each(_dot, mm, p)
    t0_hi = each(lambda x: x.astype(BF16), t0)
    resid = each(lambda l, x, x_hi: (eye - x) - _dot_split(l, x, x_hi), lowers, t0, t0_hi)
    return each(lambda x, x_hi, res: x + _dot(x_hi, res.astype(BF16)), t0, t0_hi, resid)


def _dn_prep_kernel(cur_ref, halo_ref, ba_ref, convw_ref, alog_ref, dtb_ref,
                    w_ref, qd_ref, kd_ref, u_ref, qk_ref, cd_ref):
    n = pl.program_id(1)
    r = PREP_ROWS
    cur = cur_ref[0]
    halo = jnp.where(n > 0, halo_ref[0], 0.0)
    full = jnp.concatenate([halo, cur], axis=0)
    conv = full[SUBLANES:] * convw_ref[CONV_WIDTH - 1:CONV_WIDTH, :]
    for j in range(CONV_WIDTH - 1):
        shift = CONV_WIDTH - 1 - j
        conv = conv + pltpu.roll(full, shift, axis=0)[SUBLANES:] * convw_ref[j:j + 1, :]
    qkv = _silu(conv)

    ba = ba_ref[0]
    beta_all = _sigmoid(ba)
    sp_in = ba + dtb_ref[...]
    softplus = jnp.maximum(sp_in, 0.0) + jnp.log1p(jnp.exp(-jnp.abs(sp_in)))
    g_all = -jnp.exp(alog_ref[...]) * softplus
    rows = lax.broadcasted_iota(jnp.int32, (r, r), 0)
    cols = lax.broadcasted_iota(jnp.int32, (r, r), 1)
    same_chunk = (rows // CHUNK) == (cols // CHUNK)
    causal = same_chunk & (rows >= cols)
    strict = same_chunk & (rows > cols)
    inv_block = (rows // INV_BLOCK) == (cols // INV_BLOCK)
    eye = (rows == cols).astype(F32)
    gc_all = _dot(causal.astype(F32), g_all, precision=HIGHEST)
    gc_last_all = _dot(same_chunk.astype(F32), g_all, precision=HIGHEST)
    gc_rows = gc_all.T
    chunk_decay = jnp.exp(gc_last_all)
    cd_ref[0] = jnp.concatenate(
        [chunk_decay[c * CHUNK:c * CHUNK + SUBLANES] for c in range(r // CHUNK)], axis=0)

    scale = DN_HEAD_DIM ** -0.5
    lowers, rhss, qk_parts = [], [], []
    for h in range(DN_HEADS):
        lo = h * DN_HEAD_DIM
        q = qkv[:, lo:lo + DN_HEAD_DIM]
        k = qkv[:, DN_WIDTH + lo:DN_WIDTH + lo + DN_HEAD_DIM]
        v = qkv[:, 2 * DN_WIDTH + lo:2 * DN_WIDTH + lo + DN_HEAD_DIM]
        q = q * lax.rsqrt(jnp.sum(q * q, axis=-1, keepdims=True) + NORM_EPS) * scale
        k = k * lax.rsqrt(jnp.sum(k * k, axis=-1, keepdims=True) + NORM_EPS)
        beta = beta_all[:, h:h + 1]
        gc = gc_all[:, DN_HEADS + h:DN_HEADS + h + 1]
        gc_row = gc_rows[DN_HEADS + h:DN_HEADS + h + 1, :]
        gc_last = gc_last_all[:, DN_HEADS + h:DN_HEADS + h + 1]
        decay = jnp.where(causal, jnp.exp(jnp.where(causal, gc - gc_row, 0.0)), 0.0)
        egc = jnp.exp(gc)
        k_beta = k * beta
        k16 = k.astype(BF16)
        lowers.append(jnp.where(strict, _dot_nt(k_beta.astype(BF16), k16) * decay, 0.0))
        rhss.append(jnp.concatenate([v * beta, k_beta * egc], axis=-1).astype(BF16))
        qd_ref[0, :, lo:lo + DN_HEAD_DIM] = (q * egc).astype(BF16)
        kd_ref[0, :, lo:lo + DN_HEAD_DIM] = k * jnp.exp(gc_last - gc)
        qk = _dot_nt(q.astype(BF16), k16) * decay
        compact = qk[:, 0:CHUNK]
        for c in range(1, r // CHUNK):
            compact = compact + qk[:, c * CHUNK:(c + 1) * CHUNK]
        qk_parts.append(compact.astype(BF16))
    qk_ref[0] = jnp.concatenate(qk_parts, axis=-1)

    t_invs = _unit_lower_inverses(lowers, eye, inv_block)
    for h in range(DN_HEADS):
        lo = h * DN_HEAD_DIM
        sol = _dot(t_invs[h].astype(BF16), rhss[h])
        u_ref[0, :, lo:lo + DN_HEAD_DIM] = sol[:, :DN_HEAD_DIM]
        w_ref[0, :, lo:lo + DN_HEAD_DIM] = sol[:, DN_HEAD_DIM:].astype(BF16)


def _dn_prep(qkva, ba, conv_w, alog_row, dtb_row):
    b, s, _ = qkva.shape
    r = PREP_ROWS
    n_chunks = s // CHUNK
    halo_blocks = r // SUBLANES
    fixed = lambda i, n: (0, 0)
    blk = lambda i, n: (i, n, 0)
    return pl.pallas_call(
        _dn_prep_kernel,
        grid=(b, s // r),
        in_specs=[
            pl.BlockSpec((1, r, 3 * DN_WIDTH), blk),
            pl.BlockSpec((1, SUBLANES, 3 * DN_WIDTH),
                         lambda i, n: (i, jnp.maximum(n * halo_blocks - 1, 0), 0)),
            pl.BlockSpec((1, r, LANES), blk),
            pl.BlockSpec((CONV_WIDTH, 3 * DN_WIDTH), fixed),
            pl.BlockSpec((1, LANES), fixed),
            pl.BlockSpec((1, LANES), fixed),
        ],
        out_specs=[
            pl.BlockSpec((1, r, DN_WIDTH), blk),
            pl.BlockSpec((1, r, DN_WIDTH), blk),
            pl.BlockSpec((1, r, DN_WIDTH), blk),
            pl.BlockSpec((1, r, DN_WIDTH), blk),
            pl.BlockSpec((1, r, DN_HEADS * CHUNK), blk),
            pl.BlockSpec((1, r // CHUNK * SUBLANES, LANES), blk),
        ],
        out_shape=[
            jax.ShapeDtypeStruct((b, s, DN_WIDTH), BF16),
            jax.ShapeDtypeStruct((b, s, DN_WIDTH), BF16),
            jax.ShapeDtypeStruct((b, s, DN_WIDTH), F32),
            jax.ShapeDtypeStruct((b, s, DN_WIDTH), F32),
            jax.ShapeDtypeStruct((b, s, DN_HEADS * CHUNK), BF16),
            jax.ShapeDtypeStruct((b, n_chunks * SUBLANES, LANES), F32),
        ],
        compiler_params=pltpu.CompilerParams(
            dimension_semantics=("parallel", "parallel"), vmem_limit_bytes=VMEM_LIMIT),
        name="dn_prep",
    )(qkva, qkva, ba, conv_w, alog_row, dtb_row)


def _dn_scan_kernel(w_ref, qd_ref, kd_ref, u_ref, qk_ref, cd_ref, z_ref, normw_ref, o_ref, state_ref):
    n = pl.program_id(0)

    @pl.when(n == 0)
    def _():
        state_ref[...] = jnp.zeros_like(state_ref)

    n_batch = w_ref.shape[0]
    chains = [(bi, h) for bi in range(n_batch) for h in range(DN_HEADS)]
    for g in range(0, len(chains), SCAN_GROUP):
        group = chains[g:g + SCAN_GROUP]
        col = lambda h: slice(h * DN_HEAD_DIM, (h + 1) * DN_HEAD_DIM)
        s16 = [state_ref[bi * DN_HEADS + h].astype(BF16) for bi, h in group]
        ws = [_dot(w_ref[bi, :, col(h)], s) for (bi, h), s in zip(group, s16)]
        qs = [_dot(qd_ref[bi, :, col(h)], s) for (bi, h), s in zip(group, s16)]
        vn16 = [(u_ref[bi, :, col(h)] - x).astype(BF16) for (bi, h), x in zip(group, ws)]
        inner = [_dot(qk_ref[bi, :, h * CHUNK:(h + 1) * CHUNK], v) for (bi, h), v in zip(group, vn16)]
        kd_t = [kd_ref[bi, :, col(h)].T.astype(BF16) for bi, h in group]
        upd = [_dot(k, v) for k, v in zip(kd_t, vn16)]
        for (bi, h), x in zip(group, upd):
            cd = cd_ref[bi, 0:1, DN_HEADS + h:DN_HEADS + h + 1]
            state_ref[bi * DN_HEADS + h] = state_ref[bi * DN_HEADS + h] * cd + x
        for (bi, h), a, c in zip(group, qs, inner):
            o = a + c
            o = o * lax.rsqrt(jnp.mean(o * o, axis=-1, keepdims=True) + NORM_EPS) * normw_ref[...]
            o_ref[bi, :, col(h)] = o * _silu(z_ref[bi, :, col(h)])


def _dn_scan(w, qd, kd, u, qk, cd, z, norm_w):
    b, s, _ = w.shape
    n_chunks = s // CHUNK
    blk = lambda n: (0, n, 0)
    wide = pl.BlockSpec((b, CHUNK, DN_WIDTH), blk)
    return pl.pallas_call(
        _dn_scan_kernel,
        grid=(n_chunks,),
        in_specs=[
            wide, wide, wide, wide,
            pl.BlockSpec((b, CHUNK, DN_HEADS * CHUNK), blk),
            pl.BlockSpec((b, SUBLANES, LANES), blk),
            wide,
            pl.BlockSpec((1, DN_HEAD_DIM), lambda n: (0, 0)),
        ],
        out_specs=wide,
        out_shape=jax.ShapeDtypeStruct((b, s, DN_WIDTH), F32),
        scratch_shapes=[pltpu.VMEM((b * DN_HEADS, DN_HEAD_DIM, DN_HEAD_DIM), F32)],
        compiler_params=pltpu.CompilerParams(
            dimension_semantics=("arbitrary",), vmem_limit_bytes=VMEM_LIMIT),
        name="dn_scan",
    )(w, qd, kd, u, qk, cd, z, norm_w)


def _band_attn_kernel(q_ref, k_ref, v_ref, ext_ref, nw_ref, o_ref, bias_ref):
    first = jnp.logical_and(pl.program_id(0) == 0, pl.program_id(1) == 0)

    @pl.when(first)
    def _():
        qi = lax.broadcasted_iota(jnp.int32, (ATT_ROWS, ATT_WIN), 0) // CHUNK
        kj = lax.broadcasted_iota(jnp.int32, (ATT_ROWS, ATT_WIN), 1) // CHUNK
        in_band = jnp.logical_and(kj >= qi, kj <= qi + CA_LEFT_CHUNKS)
        for h in range(CA_HEADS):
            tiled = jnp.broadcast_to(ext_ref[h:h + 1, :], (ATT_ROWS, ATT_EXT))
            toeplitz = pltpu.roll(tiled, 0, axis=1, stride=1, stride_axis=0)
            bias_ref[h] = jnp.where(in_band, toeplitz[:, :ATT_WIN], -jnp.inf)

    n0 = pl.program_id(1) * (ATT_ROWS // CHUNK)
    k_parts, v_parts = [], []
    for w in range(ATT_WIN // CHUNK):
        start = pl.multiple_of(jnp.maximum(n0 - CA_LEFT_CHUNKS + w, 0) * CHUNK, CHUNK)
        k_parts.append(k_ref[0, pl.ds(start, CHUNK), :])
        v_parts.append(v_ref[0, pl.ds(start, CHUNK), :])
    k_win = jnp.concatenate(k_parts, axis=0)
    v_win = jnp.concatenate(v_parts, axis=0)
    q = q_ref[0] * jnp.asarray(CA_HEAD_DIM ** -0.5, BF16)

    key_col = lax.broadcasted_iota(jnp.int32, (1, ATT_WIN), 1)
    before_start = key_col < (CA_LEFT_CHUNKS - n0) * CHUNK
    col_bias = jnp.where(before_start, -jnp.inf, 0.0)
    lane = lax.broadcasted_iota(jnp.int32, (ATT_ROWS, LANES), 1)
    heads_per_group = LANES // CA_HEAD_DIM

    outs = []
    for grp in range(CA_WIDTH // LANES):
        lo = grp * LANES
        qg = q[:, lo:lo + LANES]
        kg = k_win[:, lo:lo + LANES]
        vg = v_win[:, lo:lo + LANES]
        og = jnp.zeros((ATT_ROWS, LANES), F32)
        for hh in range(heads_per_group):
            head = grp * heads_per_group + hh
            in_head = (lane // CA_HEAD_DIM) == hh
            qm = jnp.where(in_head, qg, jnp.zeros_like(qg))
            s = _dot_nt(qm, kg) + bias_ref[head] + col_bias
            e = jnp.exp(s - jnp.max(s, axis=-1, keepdims=True))
            denom = jnp.sum(e, axis=-1, keepdims=True)
            og = jnp.where(in_head, _dot(e.astype(BF16), vg) / denom, og)
        outs.append(og)
    o = jnp.concatenate(outs, axis=-1)
    o = o * lax.rsqrt(jnp.mean(o * o, axis=-1, keepdims=True) + NORM_EPS) * nw_ref[...]
    o_ref[0] = o


def _band_attn(qb, kb, vb, ext, norm_w):
    b, s, _ = qb.shape
    return pl.pallas_call(
        _band_attn_kernel,
        grid=(b, s // ATT_ROWS),
        in_specs=[
            pl.BlockSpec((1, ATT_ROWS, CA_WIDTH), lambda i, n: (i, n, 0)),
            pl.BlockSpec((1, s, CA_WIDTH), lambda i, n: (i, 0, 0)),
            pl.BlockSpec((1, s, CA_WIDTH), lambda i, n: (i, 0, 0)),
            pl.BlockSpec((CA_HEADS, ATT_EXT), lambda i, n: (0, 0)),
            pl.BlockSpec((1, CA_WIDTH), lambda i, n: (0, 0)),
        ],
        out_specs=pl.BlockSpec((1, ATT_ROWS, CA_WIDTH), lambda i, n: (i, n, 0)),
        out_shape=jax.ShapeDtypeStruct((b, s, CA_WIDTH), F32),
        scratch_shapes=[pltpu.VMEM((CA_HEADS, ATT_ROWS, ATT_WIN), F32)],
        compiler_params=pltpu.CompilerParams(
            dimension_semantics=("arbitrary", "arbitrary"), vmem_limit_bytes=VMEM_LIMIT),
        name="band_attn",
    )(qb, kb, vb, ext, norm_w)


def _out_router_kernel(oa_ref, ob_ref, x_ref, wout_ref, n2_ref, wr_hi_ref, wr_lo_ref, br_ref,
                       x1_ref, h2_ref, route_ref, gates_ref, counts_ref, run_ref, earlier_ref):
    i = pl.program_id(0)
    tm = x_ref.shape[0]

    @pl.when(i == 0)
    def _():
        run_ref[...] = jnp.zeros_like(run_ref)
        r = lax.broadcasted_iota(jnp.int32, (tm, tm), 0)
        c = lax.broadcasted_iota(jnp.int32, (tm, tm), 1)
        earlier_ref[...] = (r > c).astype(BF16)

    x1 = (x_ref[...]
          + _dot(oa_ref[...].astype(BF16), wout_ref[0:DN_WIDTH, :])
          + _dot(ob_ref[...].astype(BF16), wout_ref[DN_WIDTH:, :]))
    x1_ref[...] = x1
    h2 = x1 * lax.rsqrt(jnp.mean(x1 * x1, axis=-1, keepdims=True) + NORM_EPS) * n2_ref[...]
    _store_row_slabs(h2_ref, h2, tm)
    lane = lax.broadcasted_iota(jnp.int32, (tm, LANES), 1)
    h2_hi = h2.astype(BF16)
    h2_lo = (h2 - h2_hi.astype(F32)).astype(BF16)
    logits = (_dot(h2_hi, wr_hi_ref[...]) + (_dot(h2_hi, wr_lo_ref[...]) + _dot(h2_lo, wr_hi_ref[...]))
              + br_ref[...])
    logits = jnp.where(lane < N_EXPERTS, logits, -jnp.inf)

    top_vals, top_idx, onehots = [], [], []
    for _ in range(TOP_K):
        m = jnp.max(logits, axis=-1, keepdims=True)
        idx = jnp.min(jnp.where(logits == m, lane, LANES), axis=-1, keepdims=True)
        hit = lane == idx
        top_vals.append(m)
        top_idx.append(idx)
        onehots.append(hit)
        logits = jnp.where(hit, -jnp.inf, logits)

    exps = [jnp.exp(v - top_vals[0]) for v in top_vals]
    denom = exps[0] + exps[1] + exps[2] + exps[3]
    gates = jnp.zeros((tm, LANES), F32)
    for k in range(TOP_K):
        gates = jnp.where(lane == k, exps[k] / denom, gates)
    gates_ref[...] = gates

    multi = jnp.zeros((tm, LANES), F32)
    for hit in onehots:
        multi = multi + hit.astype(F32)
    before = _dot(earlier_ref[...], multi.astype(BF16)) + run_ref[...]
    route = jnp.zeros((tm, LANES), jnp.int32)
    for k in range(TOP_K):
        rank = jnp.sum(jnp.where(onehots[k], before, 0.0), axis=-1, keepdims=True).astype(jnp.int32)
        route = jnp.where(lane == k, top_idx[k], route)
        route = jnp.where(lane == TOP_K + k, rank, route)
    route_ref[...] = route
    run_ref[...] = run_ref[...] + jnp.sum(multi, axis=0, keepdims=True)
    counts_ref[...] = run_ref[...]


def _out_router(oa, ob, x2, w_out, norm2_w, wr_hi, wr_lo, br_pad, tm):
    t, d = x2.shape
    row = lambda i: (i, 0)
    fixed = lambda i: (0, 0)
    return pl.pallas_call(
        _out_router_kernel,
        grid=(t // tm,),
        in_specs=[
            pl.BlockSpec((tm, DN_WIDTH), row),
            pl.BlockSpec((tm, CA_WIDTH), row),
            pl.BlockSpec((tm, d), row),
            pl.BlockSpec((DN_WIDTH + CA_WIDTH, d), fixed),
            pl.BlockSpec((1, d), fixed),
            pl.BlockSpec((d, LANES), fixed),
            pl.BlockSpec((d, LANES), fixed),
            pl.BlockSpec((1, LANES), fixed),
        ],
        out_specs=[
            pl.BlockSpec((tm, d), row),
            pl.BlockSpec((tm * SUBLANES, LANES), row),
            pl.BlockSpec((tm, LANES), row),
            pl.BlockSpec((tm, LANES), row),
            pl.BlockSpec((1, LANES), fixed),
        ],
        out_shape=[
            jax.ShapeDtypeStruct((t, d), F32),
            jax.ShapeDtypeStruct((t * SUBLANES, LANES), F32),
            jax.ShapeDtypeStruct((t, LANES), jnp.int32),
            jax.ShapeDtypeStruct((t, LANES), F32),
            jax.ShapeDtypeStruct((1, LANES), F32),
        ],
        scratch_shapes=[pltpu.VMEM((1, LANES), F32), pltpu.VMEM((tm, tm), BF16)],
        compiler_params=pltpu.CompilerParams(
            dimension_semantics=("arbitrary",), vmem_limit_bytes=VMEM_LIMIT),
        name="out_router",
    )(oa, ob, x2, w_out, norm2_w, wr_hi, wr_lo, br_pad)


def _invert_slots_kernel(pos_ref, fill_ref, o_ref, sem):
    i = pl.program_id(0)
    chunk = pos_ref.shape[2]

    @pl.when(i == 0)
    def _():
        fill = pltpu.make_async_copy(fill_ref, o_ref, sem)
        fill.start()
        fill.wait()

    base = i * chunk

    def place(a, carry):
        o_ref[pos_ref[0, 0, a]] = base + a
        return carry

    lax.fori_loop(0, chunk, place, 0, unroll=INVERT_UNROLL)


def _invert_slots(pos, n_slots):
    n = pos.shape[0]
    chunk = INVERT_CHUNK
    return pl.pallas_call(
        _invert_slots_kernel,
        grid=(n // chunk,),
        in_specs=[pl.BlockSpec((1, 1, chunk), lambda i: (i, 0, 0), memory_space=pltpu.SMEM),
                  pl.BlockSpec(memory_space=pl.ANY)],
        out_specs=pl.BlockSpec((n_slots,), lambda i: (0,), memory_space=pltpu.SMEM),
        out_shape=jax.ShapeDtypeStruct((n_slots,), jnp.int32),
        scratch_shapes=[pltpu.SemaphoreType.DMA],
        compiler_params=pltpu.CompilerParams(dimension_semantics=("arbitrary",)),
        name="invert_slots",
    )(pos.reshape(n // chunk, 1, chunk), jnp.full((n_slots,), -1, jnp.int32))


def _experts_kernel(be_ref, *refs):
    refs = list(refs)
    take = lambda n: [refs.pop(0) for _ in range(n)]
    src_first_refs = take(RING - 1)
    src_ahead_ref, dst_prev_ref, dst_cur_ref = take(3)
    dst_spare_refs = take(RING - 2)
    h2_ref, wg_ref, bg_ref, wu_ref, bu_ref, wd_ref, bd_ref, g_ref = take(8)
    xbufs = take(RING)
    ybufs = take(RING)
    wg16, wu16, wd16, sem_in, sem_out = refs
    b = pl.program_id(0)
    last = pl.num_programs(0) - 1
    slab = SUBLANES
    bm = xbufs[0].shape[0] // slab

    def gather_rows(idx_ref, slot):
        for r in range(bm):
            src = pl.multiple_of(idx_ref[0, 0, r], slab)
            pltpu.make_async_copy(h2_ref.at[pl.ds(src, slab)],
                                  xbufs[slot].at[pl.ds(r * slab, slab)], sem_in.at[slot]).start()

    def scatter_rows(idx_ref, slot):
        for r in range(bm):
            dst = pl.multiple_of(idx_ref[0, 0, r], OUT_SLAB)
            pltpu.make_async_copy(ybufs[slot].at[pl.ds(r * OUT_SLAB, OUT_SLAB)],
                                  g_ref.at[pl.ds(dst, OUT_SLAB)], sem_out.at[slot]).start()

    def wait_gather(slot):
        pltpu.make_async_copy(h2_ref.at[pl.ds(0, bm * slab)], xbufs[slot], sem_in.at[slot]).wait()

    def wait_scatter(slot):
        pltpu.make_async_copy(ybufs[slot], g_ref.at[pl.ds(0, bm * OUT_SLAB)], sem_out.at[slot]).wait()

    @pl.when(b == 0)
    def _():
        for slot in range(RING - 1):
            gather_rows(src_first_refs[slot], slot)
        for slot in range(1, RING):
            ybufs[slot][...] = jnp.zeros(ybufs[slot].shape, jnp.int32)
        for slot in range(1, RING - 1):
            scatter_rows(dst_spare_refs[slot - 1], slot)

    changed = jnp.logical_or(b == 0, be_ref[b] != be_ref[jnp.maximum(b - 1, 0)])

    @pl.when(changed)
    def _():
        wg16[...] = wg_ref[0].astype(BF16)
        wu16[...] = wu_ref[0].astype(BF16)
        wd16[...] = wd_ref[0].astype(BF16)

    def step(cur):
        prv = (cur + RING - 1) % RING
        nxt = (cur + 1) % RING
        wait_gather(cur)
        gather_rows(src_ahead_ref, prv)
        scatter_rows(dst_prev_ref, prv)
        x = _load_row_slabs(xbufs[cur], bm).astype(BF16)
        gate = _dot(x, wg16[...]) + bg_ref[0]
        up = _dot(x, wu16[...]) + bu_ref[0]
        gate = jnp.minimum(gate, SWIGLU_LIMIT)
        up = jnp.clip(up, -SWIGLU_LIMIT, SWIGLU_LIMIT)
        glu = gate * _sigmoid(gate * SWIGLU_ALPHA)
        mid = ((up + 1.0) * glu).astype(BF16)
        _store_row_slabs(ybufs[cur], _pack_bf16_pairs(_dot(mid, wd16[...]) + bd_ref[0]), bm)
        wait_scatter(nxt)

        @pl.when(b == last)
        def _():
            scatter_rows(dst_cur_ref, cur)
            for back in range(RING - 1):
                wait_scatter((cur + RING - back) % RING)
            for ahead in range(1, RING):
                wait_gather((cur + ahead) % RING)

    for slot in range(RING):
        pl.when(b % RING == slot)(functools.partial(step, slot))


def _experts(h2, slot_src, slot_dst, block_e, n_out_rows, wg, bg, wu, bu, wd, bd, bm):
    slab = SUBLANES
    d = wg.shape[1]
    f = wg.shape[2]
    n_blocks = slot_src.shape[0] // bm
    spare = lambda blk: n_out_rows - (RING - blk % RING) * bm + jnp.arange(bm, dtype=jnp.int32)
    src3 = (slot_src * slab).reshape(n_blocks, 1, bm)
    standins = [spare(j - RING) for j in range(1, RING - 1)]
    dst3 = (jnp.concatenate([spare(-1), slot_dst] + standins) * OUT_SLAB).reshape(n_blocks + RING - 1, 1, bm)
    smem_blk = lambda imap: pl.BlockSpec((1, 1, bm), imap, memory_space=pltpu.SMEM)
    fixed_blk = lambda j: smem_blk(lambda b, be: (j, 0, 0))
    wmap = lambda b, be: (be[b], 0, 0)
    grid_spec = pltpu.PrefetchScalarGridSpec(
        num_scalar_prefetch=1,
        grid=(n_blocks,),
        in_specs=[fixed_blk(j) for j in range(RING - 1)] + [
            smem_blk(lambda b, be: (jnp.minimum(b + RING - 1, n_blocks - 1), 0, 0)),
            smem_blk(lambda b, be: (b, 0, 0)),
            smem_blk(lambda b, be: (b + 1, 0, 0)),
        ] + [fixed_blk(n_blocks + j) for j in range(1, RING - 1)] + [
            pl.BlockSpec(memory_space=pl.ANY),
            pl.BlockSpec((1, d, f), wmap),
            pl.BlockSpec((1, 1, f), wmap),
            pl.BlockSpec((1, d, f), wmap),
            pl.BlockSpec((1, 1, f), wmap),
            pl.BlockSpec((1, f, d), wmap),
            pl.BlockSpec((1, 1, d), wmap),
        ],
        out_specs=pl.BlockSpec(memory_space=pl.ANY),
        scratch_shapes=[pltpu.VMEM((bm * slab, LANES), F32)] * RING
        + [pltpu.VMEM((bm * OUT_SLAB, LANES), jnp.int32)] * RING + [
            pltpu.VMEM((d, f), BF16),
            pltpu.VMEM((d, f), BF16),
            pltpu.VMEM((f, d), BF16),
            pltpu.SemaphoreType.DMA((RING,)),
            pltpu.SemaphoreType.DMA((RING,)),
        ],
    )
    return pl.pallas_call(
        _experts_kernel,
        grid_spec=grid_spec,
        out_shape=jax.ShapeDtypeStruct((n_out_rows * OUT_SLAB, LANES), jnp.int32),
        compiler_params=pltpu.CompilerParams(
            dimension_semantics=("arbitrary",), vmem_limit_bytes=VMEM_LIMIT),
        name="experts",
    )(block_e, *([src3] * RING), *([dst3] * RING), h2, wg, bg, wu, bu, wd, bd)


def _combine_kernel(y0_ref, y1_ref, y2_ref, y3_ref, x1_ref, gates_ref, fw_ref, o_ref):
    tm = x1_ref.shape[0]
    acc = x1_ref[...]
    gates = gates_ref[...]
    for k, y_ref in enumerate((y0_ref, y1_ref, y2_ref, y3_ref)):
        y = _unpack_bf16_pairs(_load_row_slabs(y_ref, tm))
        acc = acc + y * gates[:, k:k + 1]
    o_ref[...] = acc * lax.rsqrt(jnp.mean(acc * acc, axis=-1, keepdims=True) + NORM_EPS) * fw_ref[...]


def _combine(ys, x1, gates, final_w, tm):
    t, d = x1.shape
    n_tiles = t // tm
    row = lambda i: (i, 0)
    y_spec = lambda k: pl.BlockSpec((tm * OUT_SLAB, LANES), lambda i: (k * n_tiles + i, 0))
    return pl.pallas_call(
        _combine_kernel,
        grid=(n_tiles,),
        in_specs=[y_spec(k) for k in range(TOP_K)] + [
            pl.BlockSpec((tm, d), row),
            pl.BlockSpec((tm, LANES), row),
            pl.BlockSpec((1, d), lambda i: (0, 0)),
        ],
        out_specs=pl.BlockSpec((tm, d), row),
        out_shape=jax.ShapeDtypeStruct((t, d), F32),
        compiler_params=pltpu.CompilerParams(
            dimension_semantics=("parallel",), vmem_limit_bytes=VMEM_LIMIT),
        name="combine",
    )(ys, ys, ys, ys, x1, gates, final_w)


def _rel_bias_rows(rel_bias):
    m = np.arange(ATT_EXT)
    m = np.where(m < ATT_WIN, m, m - ATT_EXT)
    dist = CA_LEFT_CHUNKS * CHUNK - m
    idx = np.clip(dist, -MAX_REL_DIST, MAX_REL_DIST) + MAX_REL_DIST
    return rel_bias.astype(F32)[:, idx]


def _layer(x, norm1_w, w_in, conv_w, a_log, dt_bias, dn_norm_w, rel_bias, attn_norm_w, w_out,
           norm2_w, w_router, b_router, w_gate, b_gate, w_up, b_up, w_down, b_down, out_norm_w):
    b, s, d = x.shape
    assert d == SUBLANES * LANES, "row-slab layout holds one model row per (8, 128) tile"
    assert s % ATT_ROWS == 0 and s % PREP_ROWS == 0
    t = b * s
    x2 = x.reshape(t, d)

    ba_lo = 4 * DN_WIDTH
    ba_hi = ba_lo + 2 * DN_HEADS
    w_small = jnp.pad(w_in[:, ba_lo:ba_hi], ((0, 0), (0, LANES - 2 * DN_HEADS)))
    w_all = jnp.concatenate([w_in[:, :ba_lo], w_in[:, ba_hi:], w_small], axis=1).astype(BF16)
    qkva, z_a, ba, q_b, k_b, v_b = _in_proj(x2, norm1_w.reshape(1, d), w_all, tm=512)

    lane_pad = (DN_HEADS, LANES - 2 * DN_HEADS)
    alog_row = jnp.pad(a_log.astype(F32), lane_pad).reshape(1, LANES)
    dtb_row = jnp.pad(dt_bias.astype(F32), lane_pad).reshape(1, LANES)
    dn_w, dn_qd, dn_kd, dn_u, dn_qk, dn_cd = _dn_prep(
        qkva.reshape(b, s, 3 * DN_WIDTH), ba.reshape(b, s, LANES), conv_w, alog_row, dtb_row)
    o_a = _dn_scan(dn_w, dn_qd, dn_kd, dn_u, dn_qk, dn_cd, z_a.reshape(b, s, DN_WIDTH),
                   dn_norm_w.reshape(1, DN_HEAD_DIM))
    o_b = _band_attn(q_b.reshape(b, s, CA_WIDTH), k_b.reshape(b, s, CA_WIDTH),
                     v_b.reshape(b, s, CA_WIDTH), _rel_bias_rows(rel_bias),
                     attn_norm_w.reshape(1, CA_WIDTH))

    wr_pad = jnp.pad(w_router.astype(F32), ((0, 0), (0, LANES - N_EXPERTS)))
    wr_hi = wr_pad.astype(BF16)
    br_pad = jnp.pad(b_router.astype(F32), (0, LANES - N_EXPERTS)).reshape(1, LANES)
    x1, h2, route, gates, counts = _out_router(
        o_a.reshape(t, DN_WIDTH), o_b.reshape(t, CA_WIDTH), x2, w_out.astype(BF16),
        norm2_w.reshape(1, d), wr_hi, (wr_pad - wr_hi.astype(F32)).astype(BF16), br_pad, tm=512)

    bm = EXPERT_BLOCK_ROWS
    n_assign = t * TOP_K
    n_blocks = n_assign // bm + N_EXPERTS
    n_slots = n_blocks * bm
    counts_i = counts[0, :N_EXPERTS].astype(jnp.int32)
    padded = (counts_i + bm - 1) // bm * bm
    pad_end = jnp.cumsum(padded)
    pad_start = pad_end - padded
    top_e = route[:, :TOP_K]
    expert_ids = jnp.arange(N_EXPERTS, dtype=jnp.int32)
    start_of = jnp.sum(jnp.where(top_e[..., None] == expert_ids, pad_start, 0), axis=-1)
    pos = start_of + route[:, TOP_K:2 * TOP_K]
    slot_assign = _invert_slots(pos.reshape(-1), n_slots)
    n_out_rows = n_assign + RING * bm
    slot_id = jnp.arange(n_slots, dtype=jnp.int32)
    spare_row = n_assign + (slot_id // bm % RING) * bm + slot_id % bm
    is_real = slot_assign >= 0
    slot_src = jnp.where(is_real, slot_assign // TOP_K, 0)
    slot_dst = jnp.where(is_real, (slot_assign % TOP_K) * t + slot_assign // TOP_K, spare_row)
    block_start = jnp.arange(n_blocks, dtype=jnp.int32) * bm
    block_e = jnp.minimum(jnp.sum((pad_end[None, :] <= block_start[:, None]).astype(jnp.int32), axis=1),
                          N_EXPERTS - 1)

    f = w_gate.shape[-1]
    ys = _experts(h2, slot_src, slot_dst, block_e, n_out_rows,
                  w_gate, b_gate.reshape(N_EXPERTS, 1, f), w_up, b_up.reshape(N_EXPERTS, 1, f),
                  w_down, b_down.reshape(N_EXPERTS, 1, d), bm)
    out = _combine(ys, x1, gates, out_norm_w.reshape(1, d), tm=256)
    return out.reshape(b, s, d)


def kernel(x, norm1_w, w_in, conv_w, a_log, dt_bias, dn_norm_w, rel_bias, attn_norm_w, w_out, norm2_w, w_router, b_router, w_gate, b_gate, w_up, b_up, w_down, b_down, final_norm_w):
    depth = norm1_w.shape[0]
    assert depth == 1, "the final RMSNorm is fused into the last layer's combine step"
    return _layer(x, norm1_w[0], w_in[0], conv_w[0], a_log[0], dt_bias[0], dn_norm_w[0],
                  rel_bias[0], attn_norm_w[0], w_out[0], norm2_w[0], w_router[0], b_router[0],
                  w_gate[0], b_gate[0], w_up[0], b_up[0], w_down[0], b_down[0], final_norm_w)
```

```python
import functools

import jax
import numpy as np
import jax.numpy as jnp
from jax import lax
from jax.experimental import pallas as pl
from jax.experimental.pallas import tpu as pltpu

F32 = jnp.float32
BF16 = jnp.bfloat16
HIGHEST = lax.Precision.HIGHEST

NORM_EPS = 1e-6
CHUNK = 64
DN_HEADS = 4
DN_HEAD_DIM = 128
DN_WIDTH = DN_HEADS * DN_HEAD_DIM
CONV_WIDTH = 4
CA_HEADS = 8
CA_HEAD_DIM = 64
CA_WIDTH = CA_HEADS * CA_HEAD_DIM
CA_LEFT_CHUNKS = 8
MAX_REL_DIST = 256
N_EXPERTS = 32
TOP_K = 4
SWIGLU_LIMIT = 7.0
SWIGLU_ALPHA = 1.702

LANES = 128
SUBLANES = 8
INV_BLOCK = 16
PREP_ROWS = 256
SCAN_GROUP = 32
ATT_ROWS = 256
ATT_WIN = ATT_ROWS + CA_LEFT_CHUNKS * CHUNK
ATT_EXT = 1024
EXPERT_BLOCK_ROWS = 256
OUT_SLAB = 4
RING = 4
INVERT_CHUNK = 8192
INVERT_UNROLL = 8
VMEM_LIMIT = 48 * 1024 * 1024


def _dot(a, b, dims=(((1,), (0,)), ((), ())), precision=None):
    return lax.dot_general(a, b, dims, precision=precision, preferred_element_type=F32)


def _dot_nt(a, b, precision=None):
    return _dot(a, b, (((1,), (1,)), ((), ())), precision)


def _bdot(a, b):
    return _dot(a.astype(BF16), b.astype(BF16))


def _dot_split(a, b, b_hi):
    a_hi = a.astype(BF16)
    a_lo = (a - a_hi.astype(F32)).astype(BF16)
    b_lo = (b - b_hi.astype(F32)).astype(BF16)
    return _dot(a_hi, b_hi) + (_dot(a_hi, b_lo) + _dot(a_lo, b_hi))


def _store_row_slabs(ref, value, rows):
    n = value.shape[1] // LANES
    for j in range(n):
        ref[pl.ds(j, rows, stride=n), :] = value[:, j * LANES:(j + 1) * LANES]


def _load_row_slabs(ref, rows):
    n = ref.shape[0] // rows
    return jnp.concatenate([ref[pl.ds(j, rows, stride=n), :] for j in range(n)], axis=-1)


def _pack_bf16_pairs(x):
    blocks = [x[:, j:j + LANES] for j in range(0, x.shape[1], LANES)]
    pairs = [pltpu.pack_elementwise([blocks[j], blocks[j + 1]], packed_dtype=BF16)
             for j in range(0, len(blocks), 2)]
    return jnp.concatenate(pairs, axis=-1)


def _unpack_bf16_pairs(w):
    blocks = []
    for j in range(0, w.shape[1], LANES):
        for i in range(2):
            blocks.append(pltpu.unpack_elementwise(w[:, j:j + LANES], index=i, packed_dtype=BF16,
                                                   unpacked_dtype=F32))
    return jnp.concatenate(blocks, axis=-1)


def _sigmoid(x):
    return 1.0 / (1.0 + jnp.exp(-x))


def _silu(x):
    return x * _sigmoid(x)


def _in_proj_kernel(x_ref, nw_ref, w_ref, qkva_ref, z_ref, ba_ref, qb_ref, kb_ref, vb_ref):
    x = x_ref[...]
    h = x * lax.rsqrt(jnp.mean(x * x, axis=-1, keepdims=True) + NORM_EPS) * nw_ref[...]
    hb = h.astype(BF16)
    c0 = 3 * DN_WIDTH
    c1 = c0 + DN_WIDTH
    c2 = c1 + CA_WIDTH
    c3 = c2 + CA_WIDTH
    c4 = c3 + CA_WIDTH
    qkva_ref[...] = _dot(hb, w_ref[:, 0:c0])
    z_ref[...] = _dot(hb, w_ref[:, c0:c1])
    qb_ref[...] = _dot(hb, w_ref[:, c1:c2]).astype(BF16)
    kb_ref[...] = _dot(hb, w_ref[:, c2:c3]).astype(BF16)
    vb_ref[...] = _dot(hb, w_ref[:, c3:c4]).astype(BF16)
    ba_ref[...] = _dot(hb, w_ref[:, c4:c4 + LANES])


def _in_proj(x2, norm_w, w_all, tm):
    t, d = x2.shape
    wcols = w_all.shape[1]
    row = lambda i: (i, 0)
    fixed = lambda i: (0, 0)
    return pl.pallas_call(
        _in_proj_kernel,
        grid=(t // tm,),
        in_specs=[
            pl.BlockSpec((tm, d), row),
            pl.BlockSpec((1, d), fixed),
            pl.BlockSpec((d, wcols), fixed),
        ],
        out_specs=[
            pl.BlockSpec((tm, 3 * DN_WIDTH), row),
            pl.BlockSpec((tm, DN_WIDTH), row),
            pl.BlockSpec((tm, LANES), row),
            pl.BlockSpec((tm, CA_WIDTH), row),
            pl.BlockSpec((tm, CA_WIDTH), row),
            pl.BlockSpec((tm, CA_WIDTH), row),
        ],
        out_shape=[
            jax.ShapeDtypeStruct((t, 3 * DN_WIDTH), F32),
            jax.ShapeDtypeStruct((t, DN_WIDTH), F32),
            jax.ShapeDtypeStruct((t, LANES), F32),
            jax.ShapeDtypeStruct((t, CA_WIDTH), BF16),
            jax.ShapeDtypeStruct((t, CA_WIDTH), BF16),
            jax.ShapeDtypeStruct((t, CA_WIDTH), BF16),
        ],
        compiler_params=pltpu.CompilerParams(
            dimension_semantics=("parallel",), vmem_limit_bytes=VMEM_LIMIT),
        name="in_proj",
    )(x2, norm_w, w_all)


def _unit_lower_inverses(lowers, eye, inv_block):
    eye16 = eye.astype(BF16)
    bf = lambda a, b: _dot(a, b).astype(BF16)
    each = lambda fn, *lists: [fn(*args) for args in zip(*lists)]
    diag = each(lambda l: jnp.where(inv_block, l, 0.0), lowers)
    diag16 = each(lambda x: x.astype(BF16), diag)
    off16 = each(lambda l, x: (l - x).astype(BF16), lowers, diag)
    d2 = each(lambda x: bf(x, x), diag16)
    d4 = each(lambda x: bf(x, x), d2)
    d8 = each(lambda x: bf(x, x), d4)
    pa = each(lambda x, y: bf(eye16 - x, eye16 + y), diag16, d2)
    pb = each(lambda x, y: bf(eye16 + x, eye16 + y), d4, d8)
    p = each(bf, pa, pb)
    m = each(bf, p, off16)
    m2 = each(lambda x: bf(x, x), m)
    mm = each(lambda x, y: bf(eye16 - x, eye16 + y), m, m2)
    t0 = each(_dot, mm, p)
    t0_hi = each(lambda x: x.astype(BF16), t0)
    resid = each(lambda l, x, x_hi: (eye - x) - _dot_split(l, x, x_hi), lowers, t0, t0_hi)
    return each(lambda x, x_hi, res: x + _dot(x_hi, res.astype(BF16)), t0, t0_hi, resid)


def _dn_prep_kernel(cur_ref, halo_ref, ba_ref, convw_ref, alog_ref, dtb_ref,
                    w_ref, qd_ref, kd_ref, u_ref, qk_ref, cd_ref):
    n = pl.program_id(1)
    r = PREP_ROWS
    cur = cur_ref[0]
    halo = jnp.where(n > 0, halo_ref[0], 0.0)
    full = jnp.concatenate([halo, cur], axis=0)
    conv = full[SUBLANES:] * convw_ref[CONV_WIDTH - 1:CONV_WIDTH, :]
    for j in range(CONV_WIDTH - 1):
        shift = CONV_WIDTH - 1 - j
        conv = conv + pltpu.roll(full, shift, axis=0)[SUBLANES:] * convw_ref[j:j + 1, :]
    qkv = _silu(conv)

    ba = ba_ref[0]
    beta_all = _sigmoid(ba)
    sp_in = ba + dtb_ref[...]
    softplus = jnp.maximum(sp_in, 0.0) + jnp.log1p(jnp.exp(-jnp.abs(sp_in)))
    g_all = -jnp.exp(alog_ref[...]) * softplus
    rows = lax.broadcasted_iota(jnp.int32, (r, r), 0)
    cols = lax.broadcasted_iota(jnp.int32, (r, r), 1)
    same_chunk = (rows // CHUNK) == (cols // CHUNK)
    causal = same_chunk & (rows >= cols)
    strict = same_chunk & (rows > cols)
    inv_block = (rows // INV_BLOCK) == (cols // INV_BLOCK)
    eye = (rows == cols).astype(F32)
    gc_all = _dot(causal.astype(F32), g_all, precision=HIGHEST)
    gc_last_all = _dot(same_chunk.astype(F32), g_all, precision=HIGHEST)
    gc_rows = gc_all.T
    chunk_decay = jnp.exp(gc_last_all)
    cd_ref[0] = jnp.concatenate(
        [chunk_decay[c * CHUNK:c * CHUNK + SUBLANES] for c in range(r // CHUNK)], axis=0)

    scale = DN_HEAD_DIM ** -0.5
    lowers, rhss, qk_parts = [], [], []
    for h in range(DN_HEADS):
        lo = h * DN_HEAD_DIM
        q = qkv[:, lo:lo + DN_HEAD_DIM]
        k = qkv[:, DN_WIDTH + lo:DN_WIDTH + lo + DN_HEAD_DIM]
        v = qkv[:, 2 * DN_WIDTH + lo:2 * DN_WIDTH + lo + DN_HEAD_DIM]
        q = q * lax.rsqrt(jnp.sum(q * q, axis=-1, keepdims=True) + NORM_EPS) * scale
        k = k * lax.rsqrt(jnp.sum(k * k, axis=-1, keepdims=True) + NORM_EPS)
        beta = beta_all[:, h:h + 1]
        gc = gc_all[:, DN_HEADS + h:DN_HEADS + h + 1]
        gc_row = gc_rows[DN_HEADS + h:DN_HEADS + h + 1, :]
        gc_last = gc_last_all[:, DN_HEADS + h:DN_HEADS + h + 1]
        decay = jnp.where(causal, jnp.exp(jnp.where(causal, gc - gc_row, 0.0)), 0.0)
        egc = jnp.exp(gc)
        k_beta = k * beta
        k16 = k.astype(BF16)
        lowers.append(jnp.where(strict, _dot_nt(k_beta.astype(BF16), k16) * decay, 0.0))
        rhss.append(jnp.concatenate([v * beta, k_beta * egc], axis=-1).astype(BF16))
        qd_ref[0, :, lo:lo + DN_HEAD_DIM] = (q * egc).astype(BF16)
        kd_ref[0, :, lo:lo + DN_HEAD_DIM] = k * jnp.exp(gc_last - gc)
        qk = _dot_nt(q.astype(BF16), k16) * decay
        compact = qk[:, 0:CHUNK]
        for c in range(1, r // CHUNK):
            compact = compact + qk[:, c * CHUNK:(c + 1) * CHUNK]
        qk_parts.append(compact.astype(BF16))
    qk_ref[0] = jnp.concatenate(qk_parts, axis=-1)

    t_invs = _unit_lower_inverses(lowers, eye, inv_block)
    for h in range(DN_HEADS):
        lo = h * DN_HEAD_DIM
        sol = _dot(t_invs[h].astype(BF16), rhss[h])
        u_ref[0, :, lo:lo + DN_HEAD_DIM] = sol[:, :DN_HEAD_DIM]
        w_ref[0, :, lo:lo + DN_HEAD_DIM] = sol[:, DN_HEAD_DIM:].astype(BF16)


def _dn_prep(qkva, ba, conv_w, alog_row, dtb_row):
    b, s, _ = qkva.shape
    r = PREP_ROWS
    n_chunks = s // CHUNK
    halo_blocks = r // SUBLANES
    fixed = lambda i, n: (0, 0)
    blk = lambda i, n: (i, n, 0)
    return pl.pallas_call(
        _dn_prep_kernel,
        grid=(b, s // r),
        in_specs=[
            pl.BlockSpec((1, r, 3 * DN_WIDTH), blk),
            pl.BlockSpec((1, SUBLANES, 3 * DN_WIDTH),
                         lambda i, n: (i, jnp.maximum(n * halo_blocks - 1, 0), 0)),
            pl.BlockSpec((1, r, LANES), blk),
            pl.BlockSpec((CONV_WIDTH, 3 * DN_WIDTH), fixed),
            pl.BlockSpec((1, LANES), fixed),
            pl.BlockSpec((1, LANES), fixed),
        ],
        out_specs=[
            pl.BlockSpec((1, r, DN_WIDTH), blk),
            pl.BlockSpec((1, r, DN_WIDTH), blk),
            pl.BlockSpec((1, r, DN_WIDTH), blk),
            pl.BlockSpec((1, r, DN_WIDTH), blk),
            pl.BlockSpec((1, r, DN_HEADS * CHUNK), blk),
            pl.BlockSpec((1, r // CHUNK * SUBLANES, LANES), blk),
        ],
        out_shape=[
            jax.ShapeDtypeStruct((b, s, DN_WIDTH), BF16),
            jax.ShapeDtypeStruct((b, s, DN_WIDTH), BF16),
            jax.ShapeDtypeStruct((b, s, DN_WIDTH), F32),
            jax.ShapeDtypeStruct((b, s, DN_WIDTH), F32),
            jax.ShapeDtypeStruct((b, s, DN_HEADS * CHUNK), BF16),
            jax.ShapeDtypeStruct((b, n_chunks * SUBLANES, LANES), F32),
        ],
        compiler_params=pltpu.CompilerParams(
            dimension_semantics=("parallel", "parallel"), vmem_limit_bytes=VMEM_LIMIT),
        name="dn_prep",
    )(qkva, qkva, ba, conv_w, alog_row, dtb_row)


def _dn_scan_kernel(w_ref, qd_ref, kd_ref, u_ref, qk_ref, cd_ref, z_ref, normw_ref, o_ref, state_ref):
    n = pl.program_id(0)

    @pl.when(n == 0)
    def _():
        state_ref[...] = jnp.zeros_like(state_ref)

    n_batch = w_ref.shape[0]
    chains = [(bi, h) for bi in range(n_batch) for h in range(DN_HEADS)]
    for g in range(0, len(chains), SCAN_GROUP):
        group = chains[g:g + SCAN_GROUP]
        col = lambda h: slice(h * DN_HEAD_DIM, (h + 1) * DN_HEAD_DIM)
        s16 = [state_ref[bi * DN_HEADS + h].astype(BF16) for bi, h in group]
        ws = [_dot(w_ref[bi, :, col(h)], s) for (bi, h), s in zip(group, s16)]
        qs = [_dot(qd_ref[bi, :, col(h)], s) for (bi, h), s in zip(group, s16)]
        vn16 = [(u_ref[bi, :, col(h)] - x).astype(BF16) for (bi, h), x in zip(group, ws)]
        inner = [_dot(qk_ref[bi, :, h * CHUNK:(h + 1) * CHUNK], v) for (bi, h), v in zip(group, vn16)]
        kd_t = [kd_ref[bi, :, col(h)].T.astype(BF16) for bi, h in group]
        upd = [_dot(k, v) for k, v in zip(kd_t, vn16)]
        for (bi, h), x in zip(group, upd):
            cd = cd_ref[bi, 0:1, DN_HEADS + h:DN_HEADS + h + 1]
            state_ref[bi * DN_HEADS + h] = state_ref[bi * DN_HEADS + h] * cd + x
        for (bi, h), a, c in zip(group, qs, inner):
            o = a + c
            o = o * lax.rsqrt(jnp.mean(o * o, axis=-1, keepdims=True) + NORM_EPS) * normw_ref[...]
            o_ref[bi, :, col(h)] = o * _silu(z_ref[bi, :, col(h)])


def _dn_scan(w, qd, kd, u, qk, cd, z, norm_w):
    b, s, _ = w.shape
    n_chunks = s // CHUNK
    blk = lambda n: (0, n, 0)
    wide = pl.BlockSpec((b, CHUNK, DN_WIDTH), blk)
    return pl.pallas_call(
        _dn_scan_kernel,
        grid=(n_chunks,),
        in_specs=[
            wide, wide, wide, wide,
            pl.BlockSpec((b, CHUNK, DN_HEADS * CHUNK), blk),
            pl.BlockSpec((b, SUBLANES, LANES), blk),
            wide,
            pl.BlockSpec((1, DN_HEAD_DIM), lambda n: (0, 0)),
        ],
        out_specs=wide,
        out_shape=jax.ShapeDtypeStruct((b, s, DN_WIDTH), F32),
        scratch_shapes=[pltpu.VMEM((b * DN_HEADS, DN_HEAD_DIM, DN_HEAD_DIM), F32)],
        compiler_params=pltpu.CompilerParams(
            dimension_semantics=("arbitrary",), vmem_limit_bytes=VMEM_LIMIT),
        name="dn_scan",
    )(w, qd, kd, u, qk, cd, z, norm_w)


def _band_attn_kernel(q_ref, k_ref, v_ref, ext_ref, nw_ref, o_ref, bias_ref):
    first = jnp.logical_and(pl.program_id(0) == 0, pl.program_id(1) == 0)

    @pl.when(first)
    def _():
        qi = lax.broadcasted_iota(jnp.int32, (ATT_ROWS, ATT_WIN), 0) // CHUNK
        kj = lax.broadcasted_iota(jnp.int32, (ATT_ROWS, ATT_WIN), 1) // CHUNK
        in_band = jnp.logical_and(kj >= qi, kj <= qi + CA_LEFT_CHUNKS)
        for h in range(CA_HEADS):
            tiled = jnp.broadcast_to(ext_ref[h:h + 1, :], (ATT_ROWS, ATT_EXT))
            toeplitz = pltpu.roll(tiled, 0, axis=1, stride=1, stride_axis=0)
            bias_ref[h] = jnp.where(in_band, toeplitz[:, :ATT_WIN], -jnp.inf)

    n0 = pl.program_id(1) * (ATT_ROWS // CHUNK)
    k_parts, v_parts = [], []
    for w in range(ATT_WIN // CHUNK):
        start = pl.multiple_of(jnp.maximum(n0 - CA_LEFT_CHUNKS + w, 0) * CHUNK, CHUNK)
        k_parts.append(k_ref[0, pl.ds(start, CHUNK), :])
        v_parts.append(v_ref[0, pl.ds(start, CHUNK), :])
    k_win = jnp.concatenate(k_parts, axis=0)
    v_win = jnp.concatenate(v_parts, axis=0)
    q = q_ref[0] * jnp.asarray(CA_HEAD_DIM ** -0.5, BF16)

    key_col = lax.broadcasted_iota(jnp.int32, (1, ATT_WIN), 1)
    before_start = key_col < (CA_LEFT_CHUNKS - n0) * CHUNK
    col_bias = jnp.where(before_start, -jnp.inf, 0.0)
    lane = lax.broadcasted_iota(jnp.int32, (ATT_ROWS, LANES), 1)
    heads_per_group = LANES // CA_HEAD_DIM

    outs = []
    for grp in range(CA_WIDTH // LANES):
        lo = grp * LANES
        qg = q[:, lo:lo + LANES]
        kg = k_win[:, lo:lo + LANES]
        vg = v_win[:, lo:lo + LANES]
        og = jnp.zeros((ATT_ROWS, LANES), F32)
        for hh in range(heads_per_group):
            head = grp * heads_per_group + hh
            in_head = (lane // CA_HEAD_DIM) == hh
            qm = jnp.where(in_head, qg, jnp.zeros_like(qg))
            s = _dot_nt(qm, kg) + bias_ref[head] + col_bias
            e = jnp.exp(s - jnp.max(s, axis=-1, keepdims=True))
            denom = jnp.sum(e, axis=-1, keepdims=True)
            og = jnp.where(in_head, _dot(e.astype(BF16), vg) / denom, og)
        outs.append(og)
    o = jnp.concatenate(outs, axis=-1)
    o = o * lax.rsqrt(jnp.mean(o * o, axis=-1, keepdims=True) + NORM_EPS) * nw_ref[...]
    o_ref[0] = o


def _band_attn(qb, kb, vb, ext, norm_w):
    b, s, _ = qb.shape
    return pl.pallas_call(
        _band_attn_kernel,
        grid=(b, s // ATT_ROWS),
        in_specs=[
            pl.BlockSpec((1, ATT_ROWS, CA_WIDTH), lambda i, n: (i, n, 0)),
            pl.BlockSpec((1, s, CA_WIDTH), lambda i, n: (i, 0, 0)),
            pl.BlockSpec((1, s, CA_WIDTH), lambda i, n: (i, 0, 0)),
            pl.BlockSpec((CA_HEADS, ATT_EXT), lambda i, n: (0, 0)),
            pl.BlockSpec((1, CA_WIDTH), lambda i, n: (0, 0)),
        ],
        out_specs=pl.BlockSpec((1, ATT_ROWS, CA_WIDTH), lambda i, n: (i, n, 0)),
        out_shape=jax.ShapeDtypeStruct((b, s, CA_WIDTH), F32),
        scratch_shapes=[pltpu.VMEM((CA_HEADS, ATT_ROWS, ATT_WIN), F32)],
        compiler_params=pltpu.CompilerParams(
            dimension_semantics=("arbitrary", "arbitrary"), vmem_limit_bytes=VMEM_LIMIT),
        name="band_attn",
    )(qb, kb, vb, ext, norm_w)


def _out_router_kernel(oa_ref, ob_ref, x_ref, wout_ref, n2_ref, wr_hi_ref, wr_lo_ref, br_ref,
                       x1_ref, h2_ref, route_ref, gates_ref, counts_ref, run_ref, earlier_ref):
    i = pl.program_id(0)
    tm = x_ref.shape[0]

    @pl.when(i == 0)
    def _():
        run_ref[...] = jnp.zeros_like(run_ref)
        r = lax.broadcasted_iota(jnp.int32, (tm, tm), 0)
        c = lax.broadcasted_iota(jnp.int32, (tm, tm), 1)
        earlier_ref[...] = (r > c).astype(BF16)

    x1 = (x_ref[...]
          + _dot(oa_ref[...].astype(BF16), wout_ref[0:DN_WIDTH, :])
          + _dot(ob_ref[...].astype(BF16), wout_ref[DN_WIDTH:, :]))
    x1_ref[...] = x1
    h2 = x1 * lax.rsqrt(jnp.mean(x1 * x1, axis=-1, keepdims=True) + NORM_EPS) * n2_ref[...]
    _store_row_slabs(h2_ref, h2, tm)
    lane = lax.broadcasted_iota(jnp.int32, (tm, LANES), 1)
    h2_hi = h2.astype(BF16)
    h2_lo = (h2 - h2_hi.astype(F32)).astype(BF16)
    logits = (_dot(h2_hi, wr_hi_ref[...]) + (_dot(h2_hi, wr_lo_ref[...]) + _dot(h2_lo, wr_hi_ref[...]))
              + br_ref[...])
    logits = jnp.where(lane < N_EXPERTS, logits, -jnp.inf)

    top_vals, top_idx, onehots = [], [], []
    for _ in range(TOP_K):
        m = jnp.max(logits, axis=-1, keepdims=True)
        idx = jnp.min(jnp.where(logits == m, lane, LANES), axis=-1, keepdims=True)
        hit = lane == idx
        top_vals.append(m)
        top_idx.append(idx)
        onehots.append(hit)
        logits = jnp.where(hit, -jnp.inf, logits)

    exps = [jnp.exp(v - top_vals[0]) for v in top_vals]
    denom = exps[0] + exps[1] + exps[2] + exps[3]
    gates = jnp.zeros((tm, LANES), F32)
    for k in range(TOP_K):
        gates = jnp.where(lane == k, exps[k] / denom, gates)
    gates_ref[...] = gates

    multi = jnp.zeros((tm, LANES), F32)
    for hit in onehots:
        multi = multi + hit.astype(F32)
    before = _dot(earlier_ref[...], multi.astype(BF16)) + run_ref[...]
    route = jnp.zeros((tm, LANES), jnp.int32)
    for k in range(TOP_K):
        rank = jnp.sum(jnp.where(onehots[k], before, 0.0), axis=-1, keepdims=True).astype(jnp.int32)
        route = jnp.where(lane == k, top_idx[k], route)
        route = jnp.where(lane == TOP_K + k, rank, route)
    route_ref[...] = route
    run_ref[...] = run_ref[...] + jnp.sum(multi, axis=0, keepdims=True)
    counts_ref[...] = run_ref[...]


def _out_router(oa, ob, x2, w_out, norm2_w, wr_hi, wr_lo, br_pad, tm):
    t, d = x2.shape
    row = lambda i: (i, 0)
    fixed = lambda i: (0, 0)
    return pl.pallas_call(
        _out_router_kernel,
        grid=(t // tm,),
        in_specs=[
            pl.BlockSpec((tm, DN_WIDTH), row),
            pl.BlockSpec((tm, CA_WIDTH), row),
            pl.BlockSpec((tm, d), row),
            pl.BlockSpec((DN_WIDTH + CA_WIDTH, d), fixed),
            pl.BlockSpec((1, d), fixed),
            pl.BlockSpec((d, LANES), fixed),
            pl.BlockSpec((d, LANES), fixed),
            pl.BlockSpec((1, LANES), fixed),
        ],
        out_specs=[
            pl.BlockSpec((tm, d), row),
            pl.BlockSpec((tm * SUBLANES, LANES), row),
            pl.BlockSpec((tm, LANES), row),
            pl.BlockSpec((tm, LANES), row),
            pl.BlockSpec((1, LANES), fixed),
        ],
        out_shape=[
            jax.ShapeDtypeStruct((t, d), F32),
            jax.ShapeDtypeStruct((t * SUBLANES, LANES), F32),
            jax.ShapeDtypeStruct((t, LANES), jnp.int32),
            jax.ShapeDtypeStruct((t, LANES), F32),
            jax.ShapeDtypeStruct((1, LANES), F32),
        ],
        scratch_shapes=[pltpu.VMEM((1, LANES), F32), pltpu.VMEM((tm, tm), BF16)],
        compiler_params=pltpu.CompilerParams(
            dimension_semantics=("arbitrary",), vmem_limit_bytes=VMEM_LIMIT),
        name="out_router",
    )(oa, ob, x2, w_out, norm2_w, wr_hi, wr_lo, br_pad)


def _invert_slots_kernel(pos_ref, fill_ref, o_ref, sem):
    i = pl.program_id(0)
    chunk = pos_ref.shape[2]

    @pl.when(i == 0)
    def _():
        fill = pltpu.make_async_copy(fill_ref, o_ref, sem)
        fill.start()
        fill.wait()

    base = i * chunk

    def place(a, carry):
        o_ref[pos_ref[0, 0, a]] = base + a
        return carry

    lax.fori_loop(0, chunk, place, 0, unroll=INVERT_UNROLL)


def _invert_slots(pos, n_slots):
    n = pos.shape[0]
    chunk = INVERT_CHUNK
    return pl.pallas_call(
        _invert_slots_kernel,
        grid=(n // chunk,),
        in_specs=[pl.BlockSpec((1, 1, chunk), lambda i: (i, 0, 0), memory_space=pltpu.SMEM),
                  pl.BlockSpec(memory_space=pl.ANY)],
        out_specs=pl.BlockSpec((n_slots,), lambda i: (0,), memory_space=pltpu.SMEM),
        out_shape=jax.ShapeDtypeStruct((n_slots,), jnp.int32),
        scratch_shapes=[pltpu.SemaphoreType.DMA],
        compiler_params=pltpu.CompilerParams(dimension_semantics=("arbitrary",)),
        name="invert_slots",
    )(pos.reshape(n // chunk, 1, chunk), jnp.full((n_slots,), -1, jnp.int32))


def _experts_kernel(be_ref, *refs):
    refs = list(refs)
    take = lambda n: [refs.pop(0) for _ in range(n)]
    src_first_refs = take(RING - 1)
    src_ahead_ref, dst_prev_ref, dst_cur_ref = take(3)
    dst_spare_refs = take(RING - 2)
    h2_ref, wg_ref, bg_ref, wu_ref, bu_ref, wd_ref, bd_ref, g_ref = take(8)
    xbufs = take(RING)
    ybufs = take(RING)
    wg16, wu16, wd16, sem_in, sem_out = refs
    b = pl.program_id(0)
    last = pl.num_programs(0) - 1
    slab = SUBLANES
    bm = xbufs[0].shape[0] // slab

    def gather_rows(idx_ref, slot):
        for r in range(bm):
            src = pl.multiple_of(idx_ref[0, 0, r], slab)
            pltpu.make_async_copy(h2_ref.at[pl.ds(src, slab)],
                                  xbufs[slot].at[pl.ds(r * slab, slab)], sem_in.at[slot]).start()

    def scatter_rows(idx_ref, slot):
        for r in range(bm):
            dst = pl.multiple_of(idx_ref[0, 0, r], OUT_SLAB)
            pltpu.make_async_copy(ybufs[slot].at[pl.ds(r * OUT_SLAB, OUT_SLAB)],
                                  g_ref.at[pl.ds(dst, OUT_SLAB)], sem_out.at[slot]).start()

    def wait_gather(slot):
        pltpu.make_async_copy(h2_ref.at[pl.ds(0, bm * slab)], xbufs[slot], sem_in.at[slot]).wait()

    def wait_scatter(slot):
        pltpu.make_async_copy(ybufs[slot], g_ref.at[pl.ds(0, bm * OUT_SLAB)], sem_out.at[slot]).wait()

    @pl.when(b == 0)
    def _():
        for slot in range(RING - 1):
            gather_rows(src_first_refs[slot], slot)
        for slot in range(1, RING):
            ybufs[slot][...] = jnp.zeros(ybufs[slot].shape, jnp.int32)
        for slot in range(1, RING - 1):
            scatter_rows(dst_spare_refs[slot - 1], slot)

    changed = jnp.logical_or(b == 0, be_ref[b] != be_ref[jnp.maximum(b - 1, 0)])

    @pl.when(changed)
    def _():
        wg16[...] = wg_ref[0].astype(BF16)
        wu16[...] = wu_ref[0].astype(BF16)
        wd16[...] = wd_ref[0].astype(BF16)

    def step(cur):
        prv = (cur + RING - 1) % RING
        nxt = (cur + 1) % RING
        wait_gather(cur)
        gather_rows(src_ahead_ref, prv)
        scatter_rows(dst_prev_ref, prv)
        x = _load_row_slabs(xbufs[cur], bm).astype(BF16)
        gate = _dot(x, wg16[...]) + bg_ref[0]
        up = _dot(x, wu16[...]) + bu_ref[0]
        gate = jnp.minimum(gate, SWIGLU_LIMIT)
        up = jnp.clip(up, -SWIGLU_LIMIT, SWIGLU_LIMIT)
        glu = gate * _sigmoid(gate * SWIGLU_ALPHA)
        mid = ((up + 1.0) * glu).astype(BF16)
        _store_row_slabs(ybufs[cur], _pack_bf16_pairs(_dot(mid, wd16[...]) + bd_ref[0]), bm)
        wait_scatter(nxt)

        @pl.when(b == last)
        def _():
            scatter_rows(dst_cur_ref, cur)
            for back in range(RING - 1):
                wait_scatter((cur + RING - back) % RING)
            for ahead in range(1, RING):
                wait_gather((cur + ahead) % RING)

    for slot in range(RING):
        pl.when(b % RING == slot)(functools.partial(step, slot))


def _experts(h2, slot_src, slot_dst, block_e, n_out_rows, wg, bg, wu, bu, wd, bd, bm):
    slab = SUBLANES
    d = wg.shape[1]
    f = wg.shape[2]
    n_blocks = slot_src.shape[0] // bm
    spare = lambda blk: n_out_rows - (RING - blk % RING) * bm + jnp.arange(bm, dtype=jnp.int32)
    src3 = (slot_src * slab).reshape(n_blocks, 1, bm)
    standins = [spare(j - RING) for j in range(1, RING - 1)]
    dst3 = (jnp.concatenate([spare(-1), slot_dst] + standins) * OUT_SLAB).reshape(n_blocks + RING - 1, 1, bm)
    smem_blk = lambda imap: pl.BlockSpec((1, 1, bm), imap, memory_space=pltpu.SMEM)
    fixed_blk = lambda j: smem_blk(lambda b, be: (j, 0, 0))
    wmap = lambda b, be: (be[b], 0, 0)
    grid_spec = pltpu.PrefetchScalarGridSpec(
        num_scalar_prefetch=1,
        grid=(n_blocks,),
        in_specs=[fixed_blk(j) for j in range(RING - 1)] + [
            smem_blk(lambda b, be: (jnp.minimum(b + RING - 1, n_blocks - 1), 0, 0)),
            smem_blk(lambda b, be: (b, 0, 0)),
            smem_blk(lambda b, be: (b + 1, 0, 0)),
        ] + [fixed_blk(n_blocks + j) for j in range(1, RING - 1)] + [
            pl.BlockSpec(memory_space=pl.ANY),
            pl.BlockSpec((1, d, f), wmap),
            pl.BlockSpec((1, 1, f), wmap),
            pl.BlockSpec((1, d, f), wmap),
            pl.BlockSpec((1, 1, f), wmap),
            pl.BlockSpec((1, f, d), wmap),
            pl.BlockSpec((1, 1, d), wmap),
        ],
        out_specs=pl.BlockSpec(memory_space=pl.ANY),
        scratch_shapes=[pltpu.VMEM((bm * slab, LANES), F32)] * RING
        + [pltpu.VMEM((bm * OUT_SLAB, LANES), jnp.int32)] * RING + [
            pltpu.VMEM((d, f), BF16),
            pltpu.VMEM((d, f), BF16),
            pltpu.VMEM((f, d), BF16),
            pltpu.SemaphoreType.DMA((RING,)),
            pltpu.SemaphoreType.DMA((RING,)),
        ],
    )
    return pl.pallas_call(
        _experts_kernel,
        grid_spec=grid_spec,
        out_shape=jax.ShapeDtypeStruct((n_out_rows * OUT_SLAB, LANES), jnp.int32),
        compiler_params=pltpu.CompilerParams(
            dimension_semantics=("arbitrary",), vmem_limit_bytes=VMEM_LIMIT),
        name="experts",
    )(block_e, *([src3] * RING), *([dst3] * RING), h2, wg, bg, wu, bu, wd, bd)


def _combine_kernel(y0_ref, y1_ref, y2_ref, y3_ref, x1_ref, gates_ref, fw_ref, o_ref):
    tm = x1_ref.shape[0]
    acc = x1_ref[...]
    gates = gates_ref[...]
    for k, y_ref in enumerate((y0_ref, y1_ref, y2_ref, y3_ref)):
        y = _unpack_bf16_pairs(_load_row_slabs(y_ref, tm))
        acc = acc + y * gates[:, k:k + 1]
    o_ref[...] = acc * lax.rsqrt(jnp.mean(acc * acc, axis=-1, keepdims=True) + NORM_EPS) * fw_ref[...]


def _combine(ys, x1, gates, final_w, tm):
    t, d = x1.shape
    n_tiles = t // tm
    row = lambda i: (i, 0)
    y_spec = lambda k: pl.BlockSpec((tm * OUT_SLAB, LANES), lambda i: (k * n_tiles + i, 0))
    return pl.pallas_call(
        _combine_kernel,
        grid=(n_tiles,),
        in_specs=[y_spec(k) for k in range(TOP_K)] + [
            pl.BlockSpec((tm, d), row),
            pl.BlockSpec((tm, LANES), row),
            pl.BlockSpec((1, d), lambda i: (0, 0)),
        ],
        out_specs=pl.BlockSpec((tm, d), row),
        out_shape=jax.ShapeDtypeStruct((t, d), F32),
        compiler_params=pltpu.CompilerParams(
            dimension_semantics=("parallel",), vmem_limit_bytes=VMEM_LIMIT),
        name="combine",
    )(ys, ys, ys, ys, x1, gates, final_w)


def _rel_bias_rows(rel_bias):
    m = np.arange(ATT_EXT)
    m = np.where(m < ATT_WIN, m, m - ATT_EXT)
    dist = CA_LEFT_CHUNKS * CHUNK - m
    idx = np.clip(dist, -MAX_REL_DIST, MAX_REL_DIST) + MAX_REL_DIST
    return rel_bias.astype(F32)[:, idx]


def _layer(x, norm1_w, w_in, conv_w, a_log, dt_bias, dn_norm_w, rel_bias, attn_norm_w, w_out,
           norm2_w, w_router, b_router, w_gate, b_gate, w_up, b_up, w_down, b_down, out_norm_w):
    b, s, d = x.shape
    assert d == SUBLANES * LANES, "row-slab layout holds one model row per (8, 128) tile"
    assert s % ATT_ROWS == 0 and s % PREP_ROWS == 0
    t = b * s
    x2 = x.reshape(t, d)

    ba_lo = 4 * DN_WIDTH
    ba_hi = ba_lo + 2 * DN_HEADS
    w_small = jnp.pad(w_in[:, ba_lo:ba_hi], ((0, 0), (0, LANES - 2 * DN_HEADS)))
    w_all = jnp.concatenate([w_in[:, :ba_lo], w_in[:, ba_hi:], w_small], axis=1).astype(BF16)
    qkva, z_a, ba, q_b, k_b, v_b = _in_proj(x2, norm1_w.reshape(1, d), w_all, tm=512)

    lane_pad = (DN_HEADS, LANES - 2 * DN_HEADS)
    alog_row = jnp.pad(a_log.astype(F32), lane_pad).reshape(1, LANES)
    dtb_row = jnp.pad(dt_bias.astype(F32), lane_pad).reshape(1, LANES)
    dn_w, dn_qd, dn_kd, dn_u, dn_qk, dn_cd = _dn_prep(
        qkva.reshape(b, s, 3 * DN_WIDTH), ba.reshape(b, s, LANES), conv_w, alog_row, dtb_row)
    o_a = _dn_scan(dn_w, dn_qd, dn_kd, dn_u, dn_qk, dn_cd, z_a.reshape(b, s, DN_WIDTH),
                   dn_norm_w.reshape(1, DN_HEAD_DIM))
    o_b = _band_attn(q_b.reshape(b, s, CA_WIDTH), k_b.reshape(b, s, CA_WIDTH),
                     v_b.reshape(b, s, CA_WIDTH), _rel_bias_rows(rel_bias),
                     attn_norm_w.reshape(1, CA_WIDTH))

    wr_pad = jnp.pad(w_router.astype(F32), ((0, 0), (0, LANES - N_EXPERTS)))
    wr_hi = wr_pad.astype(BF16)
    br_pad = jnp.pad(b_router.astype(F32), (0, LANES - N_EXPERTS)).reshape(1, LANES)
    x1, h2, route, gates, counts = _out_router(
        o_a.reshape(t, DN_WIDTH), o_b.reshape(t, CA_WIDTH), x2, w_out.astype(BF16),
        norm2_w.reshape(1, d), wr_hi, (wr_pad - wr_hi.astype(F32)).astype(BF16), br_pad, tm=512)

    bm = EXPERT_BLOCK_ROWS
    n_assign = t * TOP_K
    n_blocks = n_assign // bm + N_EXPERTS
    n_slots = n_blocks * bm
    counts_i = counts[0, :N_EXPERTS].astype(jnp.int32)
    padded = (counts_i + bm - 1) // bm * bm
    pad_end = jnp.cumsum(padded)
    pad_start = pad_end - padded
    top_e = route[:, :TOP_K]
    expert_ids = jnp.arange(N_EXPERTS, dtype=jnp.int32)
    start_of = jnp.sum(jnp.where(top_e[..., None] == expert_ids, pad_start, 0), axis=-1)
    pos = start_of + route[:, TOP_K:2 * TOP_K]
    slot_assign = _invert_slots(pos.reshape(-1), n_slots)
    n_out_rows = n_assign + RING * bm
    slot_id = jnp.arange(n_slots, dtype=jnp.int32)
    spare_row = n_assign + (slot_id // bm % RING) * bm + slot_id % bm
    is_real = slot_assign >= 0
    slot_src = jnp.where(is_real, slot_assign // TOP_K, 0)
    slot_dst = jnp.where(is_real, (slot_assign % TOP_K) * t + slot_assign // TOP_K, spare_row)
    block_start = jnp.arange(n_blocks, dtype=jnp.int32) * bm
    block_e = jnp.minimum(jnp.sum((pad_end[None, :] <= block_start[:, None]).astype(jnp.int32), axis=1),
                          N_EXPERTS - 1)

    f = w_gate.shape[-1]
    ys = _experts(h2, slot_src, slot_dst, block_e, n_out_rows,
                  w_gate, b_gate.reshape(N_EXPERTS, 1, f), w_up, b_up.reshape(N_EXPERTS, 1, f),
                  w_down, b_down.reshape(N_EXPERTS, 1, d), bm)
    out = _combine(ys, x1, gates, out_norm_w.reshape(1, d), tm=256)
    return out.reshape(b, s, d)


def kernel(x, norm1_w, w_in, conv_w, a_log, dt_bias, dn_norm_w, rel_bias, attn_norm_w, w_out, norm2_w, w_router, b_router, w_gate, b_gate, w_up, b_up, w_down, b_down, final_norm_w):
    depth = norm1_w.shape[0]
    assert depth == 1, "the final RMSNorm is fused into the last layer's combine step"
    return _layer(x, norm1_w[0], w_in[0], conv_w[0], a_log[0], dt_bias[0], dn_norm_w[0],
                  rel_bias[0], attn_norm_w[0], w_out[0], norm2_w[0], w_router[0], b_router[0],
                  w_gate[0], b_gate[0], w_up[0], b_up[0], w_down[0], b_down[0], final_norm_w)
```

```python
import functools

import jax
import numpy as np
import jax.numpy as jnp
from jax import lax
from jax.experimental import pallas as pl
from jax.experimental.pallas import tpu as pltpu

F32 = jnp.float32
BF16 = jnp.bfloat16
HIGHEST = lax.Precision.HIGHEST

NORM_EPS = 1e-6
CHUNK = 64
DN_HEADS = 4
DN_HEAD_DIM = 128
DN_WIDTH = DN_HEADS * DN_HEAD_DIM
CONV_WIDTH = 4
CA_HEADS = 8
CA_HEAD_DIM = 64
CA_WIDTH = CA_HEADS * CA_HEAD_DIM
CA_LEFT_CHUNKS = 8
MAX_REL_DIST = 256
N_EXPERTS = 32
TOP_K = 4
SWIGLU_LIMIT = 7.0
SWIGLU_ALPHA = 1.702

LANES = 128
SUBLANES = 8
INV_BLOCK = 16
PREP_ROWS = 256
SCAN_GROUP = 32
ATT_ROWS = 256
ATT_WIN = ATT_ROWS + CA_LEFT_CHUNKS * CHUNK
ATT_EXT = 1024
EXPERT_BLOCK_ROWS = 256
OUT_SLAB = 4
RING = 6
INVERT_CHUNK = 8192
INVERT_UNROLL = 8
VMEM_LIMIT = 48 * 1024 * 1024


def _dot(a, b, dims=(((1,), (0,)), ((), ())), precision=None):
    return lax.dot_general(a, b, dims, precision=precision, preferred_element_type=F32)


def _dot_nt(a, b, precision=None):
    return _dot(a, b, (((1,), (1,)), ((), ())), precision)


def _bdot(a, b):
    return _dot(a.astype(BF16), b.astype(BF16))


def _dot_split(a, b, b_hi):
    a_hi = a.astype(BF16)
    a_lo = (a - a_hi.astype(F32)).astype(BF16)
    b_lo = (b - b_hi.astype(F32)).astype(BF16)
    return _dot(a_hi, b_hi) + (_dot(a_hi, b_lo) + _dot(a_lo, b_hi))


def _store_row_slabs(ref, value, rows):
    n = value.shape[1] // LANES
    for j in range(n):
        ref[pl.ds(j, rows, stride=n), :] = value[:, j * LANES:(j + 1) * LANES]


def _load_row_slabs(ref, rows):
    n = ref.shape[0] // rows
    return jnp.concatenate([ref[pl.ds(j, rows, stride=n), :] for j in range(n)], axis=-1)


def _pack_bf16_pairs(x):
    blocks = [x[:, j:j + LANES] for j in range(0, x.shape[1], LANES)]
    pairs = [pltpu.pack_elementwise([blocks[j], blocks[j + 1]], packed_dtype=BF16)
             for j in range(0, len(blocks), 2)]
    return jnp.concatenate(pairs, axis=-1)


def _unpack_bf16_pairs(w):
    blocks = []
    for j in range(0, w.shape[1], LANES):
        for i in range(2):
            blocks.append(pltpu.unpack_elementwise(w[:, j:j + LANES], index=i, packed_dtype=BF16,
                                                   unpacked_dtype=F32))
    return jnp.concatenate(blocks, axis=-1)


def _sigmoid(x):
    return 1.0 / (1.0 + jnp.exp(-x))


def _silu(x):
    return x * _sigmoid(x)


def _in_proj_kernel(x_ref, nw_ref, w_ref, qkva_ref, z_ref, ba_ref, qb_ref, kb_ref, vb_ref):
    x = x_ref[...]
    h = x * lax.rsqrt(jnp.mean(x * x, axis=-1, keepdims=True) + NORM_EPS) * nw_ref[...]
    hb = h.astype(BF16)
    c0 = 3 * DN_WIDTH
    c1 = c0 + DN_WIDTH
    c2 = c1 + CA_WIDTH
    c3 = c2 + CA_WIDTH
    c4 = c3 + CA_WIDTH
    qkva_ref[...] = _dot(hb, w_ref[:, 0:c0])
    z_ref[...] = _dot(hb, w_ref[:, c0:c1])
    qb_ref[...] = _dot(hb, w_ref[:, c1:c2]).astype(BF16)
    kb_ref[...] = _dot(hb, w_ref[:, c2:c3]).astype(BF16)
    vb_ref[...] = _dot(hb, w_ref[:, c3:c4]).astype(BF16)
    ba_ref[...] = _dot(hb, w_ref[:, c4:c4 + LANES])


def _in_proj(x2, norm_w, w_all, tm):
    t, d = x2.shape
    wcols = w_all.shape[1]
    row = lambda i: (i, 0)
    fixed = lambda i: (0, 0)
    return pl.pallas_call(
        _in_proj_kernel,
        grid=(t // tm,),
        in_specs=[
            pl.BlockSpec((tm, d), row),
            pl.BlockSpec((1, d), fixed),
            pl.BlockSpec((d, wcols), fixed),
        ],
        out_specs=[
            pl.BlockSpec((tm, 3 * DN_WIDTH), row),
            pl.BlockSpec((tm, DN_WIDTH), row),
            pl.BlockSpec((tm, LANES), row),
            pl.BlockSpec((tm, CA_WIDTH), row),
            pl.BlockSpec((tm, CA_WIDTH), row),
            pl.BlockSpec((tm, CA_WIDTH), row),
        ],
        out_shape=[
            jax.ShapeDtypeStruct((t, 3 * DN_WIDTH), F32),
            jax.ShapeDtypeStruct((t, DN_WIDTH), F32),
            jax.ShapeDtypeStruct((t, LANES), F32),
            jax.ShapeDtypeStruct((t, CA_WIDTH), BF16),
            jax.ShapeDtypeStruct((t, CA_WIDTH), BF16),
            jax.ShapeDtypeStruct((t, CA_WIDTH), BF16),
        ],
        compiler_params=pltpu.CompilerParams(
            dimension_semantics=("parallel",), vmem_limit_bytes=VMEM_LIMIT),
        name="in_proj",
    )(x2, norm_w, w_all)


def _unit_lower_inverses(lowers, eye, inv_block):
    eye16 = eye.astype(BF16)
    bf = lambda a, b: _dot(a, b).astype(BF16)
    each = lambda fn, *lists: [fn(*args) for args in zip(*lists)]
    diag = each(lambda l: jnp.where(inv_block, l, 0.0), lowers)
    diag16 = each(lambda x: x.astype(BF16), diag)
    off16 = each(lambda l, x: (l - x).astype(BF16), lowers, diag)
    d2 = each(lambda x: bf(x, x), diag16)
    d4 = each(lambda x: bf(x, x), d2)
    d8 = each(lambda x: bf(x, x), d4)
    pa = each(lambda x, y: bf(eye16 - x, eye16 + y), diag16, d2)
    pb = each(lambda x, y: bf(eye16 + x, eye16 + y), d4, d8)
    p = each(bf, pa, pb)
    m = each(bf, p, off16)
    m2 = each(lambda x: bf(x, x), m)
    mm = each(lambda x, y: bf(eye16 - x, eye16 + y), m, m2)
    t0 = each(_dot, mm, p)
    t0_hi = each(lambda x: x.astype(BF16), t0)
    resid = each(lambda l, x, x_hi: (eye - x) - _dot_split(l, x, x_hi), lowers, t0, t0_hi)
    return each(lambda x, x_hi, res: x + _dot(x_hi, res.astype(BF16)), t0, t0_hi, resid)


def _dn_prep_kernel(cur_ref, halo_ref, ba_ref, convw_ref, alog_ref, dtb_ref,
                    w_ref, qd_ref, kd_ref, u_ref, qk_ref, cd_ref):
    n = pl.program_id(1)
    r = PREP_ROWS
    cur = cur_ref[0]
    halo = jnp.where(n > 0, halo_ref[0], 0.0)
    full = jnp.concatenate([halo, cur], axis=0)
    conv = full[SUBLANES:] * convw_ref[CONV_WIDTH - 1:CONV_WIDTH, :]
    for j in range(CONV_WIDTH - 1):
        shift = CONV_WIDTH - 1 - j
        conv = conv + pltpu.roll(full, shift, axis=0)[SUBLANES:] * convw_ref[j:j + 1, :]
    qkv = _silu(conv)

    ba = ba_ref[0]
    beta_all = _sigmoid(ba)
    sp_in = ba + dtb_ref[...]
    softplus = jnp.maximum(sp_in, 0.0) + jnp.log1p(jnp.exp(-jnp.abs(sp_in)))
    g_all = -jnp.exp(alog_ref[...]) * softplus
    rows = lax.broadcasted_iota(jnp.int32, (r, r), 0)
    cols = lax.broadcasted_iota(jnp.int32, (r, r), 1)
    same_chunk = (rows // CHUNK) == (cols // CHUNK)
    causal = same_chunk & (rows >= cols)
    strict = same_chunk & (rows > cols)
    inv_block = (rows // INV_BLOCK) == (cols // INV_BLOCK)
    eye = (rows == cols).astype(F32)
    gc_all = _dot(causal.astype(F32), g_all, precision=HIGHEST)
    gc_last_all = _dot(same_chunk.astype(F32), g_all, precision=HIGHEST)
    gc_rows = gc_all.T
    chunk_decay = jnp.exp(gc_last_all)
    cd_ref[0] = jnp.concatenate(
        [chunk_decay[c * CHUNK:c * CHUNK + SUBLANES] for c in range(r // CHUNK)], axis=0)

    scale = DN_HEAD_DIM ** -0.5
    lowers, rhss, qk_parts = [], [], []
    for h in range(DN_HEADS):
        lo = h * DN_HEAD_DIM
        q = qkv[:, lo:lo + DN_HEAD_DIM]
        k = qkv[:, DN_WIDTH + lo:DN_WIDTH + lo + DN_HEAD_DIM]
        v = qkv[:, 2 * DN_WIDTH + lo:2 * DN_WIDTH + lo + DN_HEAD_DIM]
        q = q * lax.rsqrt(jnp.sum(q * q, axis=-1, keepdims=True) + NORM_EPS) * scale
        k = k * lax.rsqrt(jnp.sum(k * k, axis=-1, keepdims=True) + NORM_EPS)
        beta = beta_all[:, h:h + 1]
        gc = gc_all[:, DN_HEADS + h:DN_HEADS + h + 1]
        gc_row = gc_rows[DN_HEADS + h:DN_HEADS + h + 1, :]
        gc_last = gc_last_all[:, DN_HEADS + h:DN_HEADS + h + 1]
        decay = jnp.where(causal, jnp.exp(jnp.where(causal, gc - gc_row, 0.0)), 0.0)
        egc = jnp.exp(gc)
        k_beta = k * beta
        k16 = k.astype(BF16)
        lowers.append(jnp.where(strict, _dot_nt(k_beta.astype(BF16), k16) * decay, 0.0))
        rhss.append(jnp.concatenate([v * beta, k_beta * egc], axis=-1).astype(BF16))
        qd_ref[0, :, lo:lo + DN_HEAD_DIM] = (q * egc).astype(BF16)
        kd_ref[0, :, lo:lo + DN_HEAD_DIM] = k * jnp.exp(gc_last - gc)
        qk = _dot_nt(q.astype(BF16), k16) * decay
        compact = qk[:, 0:CHUNK]
        for c in range(1, r // CHUNK):
            compact = compact + qk[:, c * CHUNK:(c + 1) * CHUNK]
        qk_parts.append(compact.astype(BF16))
    qk_ref[0] = jnp.concatenate(qk_parts, axis=-1)

    t_invs = _unit_lower_inverses(lowers, eye, inv_block)
    for h in range(DN_HEADS):
        lo = h * DN_HEAD_DIM
        sol = _dot(t_invs[h].astype(BF16), rhss[h])
        u_ref[0, :, lo:lo + DN_HEAD_DIM] = sol[:, :DN_HEAD_DIM]
        w_ref[0, :, lo:lo + DN_HEAD_DIM] = sol[:, DN_HEAD_DIM:].astype(BF16)


def _dn_prep(qkva, ba, conv_w, alog_row, dtb_row):
    b, s, _ = qkva.shape
    r = PREP_ROWS
    n_chunks = s // CHUNK
    halo_blocks = r // SUBLANES
    fixed = lambda i, n: (0, 0)
    blk = lambda i, n: (i, n, 0)
    return pl.pallas_call(
        _dn_prep_kernel,
        grid=(b, s // r),
        in_specs=[
            pl.BlockSpec((1, r, 3 * DN_WIDTH), blk),
            pl.BlockSpec((1, SUBLANES, 3 * DN_WIDTH),
                         lambda i, n: (i, jnp.maximum(n * halo_blocks - 1, 0), 0)),
            pl.BlockSpec((1, r, LANES), blk),
            pl.BlockSpec((CONV_WIDTH, 3 * DN_WIDTH), fixed),
            pl.BlockSpec((1, LANES), fixed),
            pl.BlockSpec((1, LANES), fixed),
        ],
        out_specs=[
            pl.BlockSpec((1, r, DN_WIDTH), blk),
            pl.BlockSpec((1, r, DN_WIDTH), blk),
            pl.BlockSpec((1, r, DN_WIDTH), blk),
            pl.BlockSpec((1, r, DN_WIDTH), blk),
            pl.BlockSpec((1, r, DN_HEADS * CHUNK), blk),
            pl.BlockSpec((1, r // CHUNK * SUBLANES, LANES), blk),
        ],
        out_shape=[
            jax.ShapeDtypeStruct((b, s, DN_WIDTH), BF16),
            jax.ShapeDtypeStruct((b, s, DN_WIDTH), BF16),
            jax.ShapeDtypeStruct((b, s, DN_WIDTH), F32),
            jax.ShapeDtypeStruct((b, s, DN_WIDTH), F32),
            jax.ShapeDtypeStruct((b, s, DN_HEADS * CHUNK), BF16),
            jax.ShapeDtypeStruct((b, n_chunks * SUBLANES, LANES), F32),
        ],
        compiler_params=pltpu.CompilerParams(
            dimension_semantics=("parallel", "parallel"), vmem_limit_bytes=VMEM_LIMIT),
        name="dn_prep",
    )(qkva, qkva, ba, conv_w, alog_row, dtb_row)


def _dn_scan_kernel(w_ref, qd_ref, kd_ref, u_ref, qk_ref, cd_ref, z_ref, normw_ref, o_ref, state_ref):
    n = pl.program_id(0)

    @pl.when(n == 0)
    def _():
        state_ref[...] = jnp.zeros_like(state_ref)

    n_batch = w_ref.shape[0]
    chains = [(bi, h) for bi in range(n_batch) for h in range(DN_HEADS)]
    for g in range(0, len(chains), SCAN_GROUP):
        group = chains[g:g + SCAN_GROUP]
        col = lambda h: slice(h * DN_HEAD_DIM, (h + 1) * DN_HEAD_DIM)
        s16 = [state_ref[bi * DN_HEADS + h].astype(BF16) for bi, h in group]
        ws = [_dot(w_ref[bi, :, col(h)], s) for (bi, h), s in zip(group, s16)]
        qs = [_dot(qd_ref[bi, :, col(h)], s) for (bi, h), s in zip(group, s16)]
        vn16 = [(u_ref[bi, :, col(h)] - x).astype(BF16) for (bi, h), x in zip(group, ws)]
        inner = [_dot(qk_ref[bi, :, h * CHUNK:(h + 1) * CHUNK], v) for (bi, h), v in zip(group, vn16)]
        kd_t = [kd_ref[bi, :, col(h)].T.astype(BF16) for bi, h in group]
        upd = [_dot(k, v) for k, v in zip(kd_t, vn16)]
        for (bi, h), x in zip(group, upd):
            cd = cd_ref[bi, 0:1, DN_HEADS + h:DN_HEADS + h + 1]
            state_ref[bi * DN_HEADS + h] = state_ref[bi * DN_HEADS + h] * cd + x
        for (bi, h), a, c in zip(group, qs, inner):
            o = a + c
            o = o * lax.rsqrt(jnp.mean(o * o, axis=-1, keepdims=True) + NORM_EPS) * normw_ref[...]
            o_ref[bi, :, col(h)] = o * _silu(z_ref[bi, :, col(h)])


def _dn_scan(w, qd, kd, u, qk, cd, z, norm_w):
    b, s, _ = w.shape
    n_chunks = s // CHUNK
    blk = lambda n: (0, n, 0)
    wide = pl.BlockSpec((b, CHUNK, DN_WIDTH), blk)
    return pl.pallas_call(
        _dn_scan_kernel,
        grid=(n_chunks,),
        in_specs=[
            wide, wide, wide, wide,
            pl.BlockSpec((b, CHUNK, DN_HEADS * CHUNK), blk),
            pl.BlockSpec((b, SUBLANES, LANES), blk),
            wide,
            pl.BlockSpec((1, DN_HEAD_DIM), lambda n: (0, 0)),
        ],
        out_specs=wide,
        out_shape=jax.ShapeDtypeStruct((b, s, DN_WIDTH), F32),
        scratch_shapes=[pltpu.VMEM((b * DN_HEADS, DN_HEAD_DIM, DN_HEAD_DIM), F32)],
        compiler_params=pltpu.CompilerParams(
            dimension_semantics=("arbitrary",), vmem_limit_bytes=VMEM_LIMIT),
        name="dn_scan",
    )(w, qd, kd, u, qk, cd, z, norm_w)


def _band_attn_kernel(q_ref, k_ref, v_ref, ext_ref, nw_ref, o_ref, bias_ref):
    first = jnp.logical_and(pl.program_id(0) == 0, pl.program_id(1) == 0)

    @pl.when(first)
    def _():
        qi = lax.broadcasted_iota(jnp.int32, (ATT_ROWS, ATT_WIN), 0) // CHUNK
        kj = lax.broadcasted_iota(jnp.int32, (ATT_ROWS, ATT_WIN), 1) // CHUNK
        in_band = jnp.logical_and(kj >= qi, kj <= qi + CA_LEFT_CHUNKS)
        for h in range(CA_HEADS):
            tiled = jnp.broadcast_to(ext_ref[h:h + 1, :], (ATT_ROWS, ATT_EXT))
            toeplitz = pltpu.roll(tiled, 0, axis=1, stride=1, stride_axis=0)
            bias_ref[h] = jnp.where(in_band, toeplitz[:, :ATT_WIN], -jnp.inf)

    n0 = pl.program_id(1) * (ATT_ROWS // CHUNK)
    k_parts, v_parts = [], []
    for w in range(ATT_WIN // CHUNK):
        start = pl.multiple_of(jnp.maximum(n0 - CA_LEFT_CHUNKS + w, 0) * CHUNK, CHUNK)
        k_parts.append(k_ref[0, pl.ds(start, CHUNK), :])
        v_parts.append(v_ref[0, pl.ds(start, CHUNK), :])
    k_win = jnp.concatenate(k_parts, axis=0)
    v_win = jnp.concatenate(v_parts, axis=0)
    q = q_ref[0] * jnp.asarray(CA_HEAD_DIM ** -0.5, BF16)

    key_col = lax.broadcasted_iota(jnp.int32, (1, ATT_WIN), 1)
    before_start = key_col < (CA_LEFT_CHUNKS - n0) * CHUNK
    col_bias = jnp.where(before_start, -jnp.inf, 0.0)
    lane = lax.broadcasted_iota(jnp.int32, (ATT_ROWS, LANES), 1)
    heads_per_group = LANES // CA_HEAD_DIM

    outs = []
    for grp in range(CA_WIDTH // LANES):
        lo = grp * LANES
        qg = q[:, lo:lo + LANES]
        kg = k_win[:, lo:lo + LANES]
        vg = v_win[:, lo:lo + LANES]
        og = jnp.zeros((ATT_ROWS, LANES), F32)
        for hh in range(heads_per_group):
            head = grp * heads_per_group + hh
            in_head = (lane // CA_HEAD_DIM) == hh
            qm = jnp.where(in_head, qg, jnp.zeros_like(qg))
            s = _dot_nt(qm, kg) + bias_ref[head] + col_bias
            e = jnp.exp(s - jnp.max(s, axis=-1, keepdims=True))
            denom = jnp.sum(e, axis=-1, keepdims=True)
            og = jnp.where(in_head, _dot(e.astype(BF16), vg) / denom, og)
        outs.append(og)
    o = jnp.concatenate(outs, axis=-1)
    o = o * lax.rsqrt(jnp.mean(o * o, axis=-1, keepdims=True) + NORM_EPS) * nw_ref[...]
    o_ref[0] = o


def _band_attn(qb, kb, vb, ext, norm_w):
    b, s, _ = qb.shape
    return pl.pallas_call(
        _band_attn_kernel,
        grid=(b, s // ATT_ROWS),
        in_specs=[
            pl.BlockSpec((1, ATT_ROWS, CA_WIDTH), lambda i, n: (i, n, 0)),
            pl.BlockSpec((1, s, CA_WIDTH), lambda i, n: (i, 0, 0)),
            pl.BlockSpec((1, s, CA_WIDTH), lambda i, n: (i, 0, 0)),
            pl.BlockSpec((CA_HEADS, ATT_EXT), lambda i, n: (0, 0)),
            pl.BlockSpec((1, CA_WIDTH), lambda i, n: (0, 0)),
        ],
        out_specs=pl.BlockSpec((1, ATT_ROWS, CA_WIDTH), lambda i, n: (i, n, 0)),
        out_shape=jax.ShapeDtypeStruct((b, s, CA_WIDTH), F32),
        scratch_shapes=[pltpu.VMEM((CA_HEADS, ATT_ROWS, ATT_WIN), F32)],
        compiler_params=pltpu.CompilerParams(
            dimension_semantics=("arbitrary", "arbitrary"), vmem_limit_bytes=VMEM_LIMIT),
        name="band_attn",
    )(qb, kb, vb, ext, norm_w)


def _out_router_kernel(oa_ref, ob_ref, x_ref, wout_ref, n2_ref, wr_hi_ref, wr_lo_ref, br_ref,
                       x1_ref, h2_ref, route_ref, gates_ref, counts_ref, run_ref, earlier_ref):
    i = pl.program_id(0)
    tm = x_ref.shape[0]

    @pl.when(i == 0)
    def _():
        run_ref[...] = jnp.zeros_like(run_ref)
        r = lax.broadcasted_iota(jnp.int32, (tm, tm), 0)
        c = lax.broadcasted_iota(jnp.int32, (tm, tm), 1)
        earlier_ref[...] = (r > c).astype(BF16)

    x1 = (x_ref[...]
          + _dot(oa_ref[...].astype(BF16), wout_ref[0:DN_WIDTH, :])
          + _dot(ob_ref[...].astype(BF16), wout_ref[DN_WIDTH:, :]))
    x1_ref[...] = x1
    h2 = x1 * lax.rsqrt(jnp.mean(x1 * x1, axis=-1, keepdims=True) + NORM_EPS) * n2_ref[...]
    _store_row_slabs(h2_ref, h2, tm)
    lane = lax.broadcasted_iota(jnp.int32, (tm, LANES), 1)
    h2_hi = h2.astype(BF16)
    h2_lo = (h2 - h2_hi.astype(F32)).astype(BF16)
    logits = (_dot(h2_hi, wr_hi_ref[...]) + (_dot(h2_hi, wr_lo_ref[...]) + _dot(h2_lo, wr_hi_ref[...]))
              + br_ref[...])
    logits = jnp.where(lane < N_EXPERTS, logits, -jnp.inf)

    top_vals, top_idx, onehots = [], [], []
    for _ in range(TOP_K):
        m = jnp.max(logits, axis=-1, keepdims=True)
        idx = jnp.min(jnp.where(logits == m, lane, LANES), axis=-1, keepdims=True)
        hit = lane == idx
        top_vals.append(m)
        top_idx.append(idx)
        onehots.append(hit)
        logits = jnp.where(hit, -jnp.inf, logits)

    exps = [jnp.exp(v - top_vals[0]) for v in top_vals]
    denom = exps[0] + exps[1] + exps[2] + exps[3]
    gates = jnp.zeros((tm, LANES), F32)
    for k in range(TOP_K):
        gates = jnp.where(lane == k, exps[k] / denom, gates)
    gates_ref[...] = gates

    multi = jnp.zeros((tm, LANES), F32)
    for hit in onehots:
        multi = multi + hit.astype(F32)
    before = _dot(earlier_ref[...], multi.astype(BF16)) + run_ref[...]
    route = jnp.zeros((tm, LANES), jnp.int32)
    for k in range(TOP_K):
        rank = jnp.sum(jnp.where(onehots[k], before, 0.0), axis=-1, keepdims=True).astype(jnp.int32)
        route = jnp.where(lane == k, top_idx[k], route)
        route = jnp.where(lane == TOP_K + k, rank, route)
    route_ref[...] = route
    run_ref[...] = run_ref[...] + jnp.sum(multi, axis=0, keepdims=True)
    counts_ref[...] = run_ref[...]


def _out_router(oa, ob, x2, w_out, norm2_w, wr_hi, wr_lo, br_pad, tm):
    t, d = x2.shape
    row = lambda i: (i, 0)
    fixed = lambda i: (0, 0)
    return pl.pallas_call(
        _out_router_kernel,
        grid=(t // tm,),
        in_specs=[
            pl.BlockSpec((tm, DN_WIDTH), row),
            pl.BlockSpec((tm, CA_WIDTH), row),
            pl.BlockSpec((tm, d), row),
            pl.BlockSpec((DN_WIDTH + CA_WIDTH, d), fixed),
            pl.BlockSpec((1, d), fixed),
            pl.BlockSpec((d, LANES), fixed),
            pl.BlockSpec((d, LANES), fixed),
            pl.BlockSpec((1, LANES), fixed),
        ],
        out_specs=[
            pl.BlockSpec((tm, d), row),
            pl.BlockSpec((tm * SUBLANES, LANES), row),
            pl.BlockSpec((tm, LANES), row),
            pl.BlockSpec((tm, LANES), row),
            pl.BlockSpec((1, LANES), fixed),
        ],
        out_shape=[
            jax.ShapeDtypeStruct((t, d), F32),
            jax.ShapeDtypeStruct((t * SUBLANES, LANES), F32),
            jax.ShapeDtypeStruct((t, LANES), jnp.int32),
            jax.ShapeDtypeStruct((t, LANES), F32),
            jax.ShapeDtypeStruct((1, LANES), F32),
        ],
        scratch_shapes=[pltpu.VMEM((1, LANES), F32), pltpu.VMEM((tm, tm), BF16)],
        compiler_params=pltpu.CompilerParams(
            dimension_semantics=("arbitrary",), vmem_limit_bytes=VMEM_LIMIT),
        name="out_router",
    )(oa, ob, x2, w_out, norm2_w, wr_hi, wr_lo, br_pad)


def _invert_slots_kernel(pos_ref, fill_ref, o_ref, sem):
    i = pl.program_id(0)
    chunk = pos_ref.shape[2]

    @pl.when(i == 0)
    def _():
        fill = pltpu.make_async_copy(fill_ref, o_ref, sem)
        fill.start()
        fill.wait()

    base = i * chunk

    def place(a, carry):
        o_ref[pos_ref[0, 0, a]] = base + a
        return carry

    lax.fori_loop(0, chunk, place, 0, unroll=INVERT_UNROLL)


def _invert_slots(pos, n_slots):
    n = pos.shape[0]
    chunk = INVERT_CHUNK
    return pl.pallas_call(
        _invert_slots_kernel,
        grid=(n // chunk,),
        in_specs=[pl.BlockSpec((1, 1, chunk), lambda i: (i, 0, 0), memory_space=pltpu.SMEM),
                  pl.BlockSpec(memory_space=pl.ANY)],
        out_specs=pl.BlockSpec((n_slots,), lambda i: (0,), memory_space=pltpu.SMEM),
        out_shape=jax.ShapeDtypeStruct((n_slots,), jnp.int32),
        scratch_shapes=[pltpu.SemaphoreType.DMA],
        compiler_params=pltpu.CompilerParams(dimension_semantics=("arbitrary",)),
        name="invert_slots",
    )(pos.reshape(n // chunk, 1, chunk), jnp.full((n_slots,), -1, jnp.int32))


def _experts_kernel(be_ref, *refs):
    refs = list(refs)
    take = lambda n: [refs.pop(0) for _ in range(n)]
    src_first_refs = take(RING - 1)
    src_ahead_ref, dst_prev_ref, dst_cur_ref = take(3)
    dst_spare_refs = take(RING - 2)
    h2_ref, wg_ref, bg_ref, wu_ref, bu_ref, wd_ref, bd_ref, g_ref = take(8)
    xbufs = take(RING)
    ybufs = take(RING)
    wg16, wu16, wd16, sem_in, sem_out = refs
    b = pl.program_id(0)
    last = pl.num_programs(0) - 1
    slab = SUBLANES
    bm = xbufs[0].shape[0] // slab

    def gather_rows(idx_ref, slot):
        for r in range(bm):
            src = pl.multiple_of(idx_ref[0, 0, r], slab)
            pltpu.make_async_copy(h2_ref.at[pl.ds(src, slab)],
                                  xbufs[slot].at[pl.ds(r * slab, slab)], sem_in.at[slot]).start()

    def scatter_rows(idx_ref, slot):
        for r in range(bm):
            dst = pl.multiple_of(idx_ref[0, 0, r], OUT_SLAB)
            pltpu.make_async_copy(ybufs[slot].at[pl.ds(r * OUT_SLAB, OUT_SLAB)],
                                  g_ref.at[pl.ds(dst, OUT_SLAB)], sem_out.at[slot]).start()

    def wait_gather(slot):
        pltpu.make_async_copy(h2_ref.at[pl.ds(0, bm * slab)], xbufs[slot], sem_in.at[slot]).wait()

    def wait_scatter(slot):
        pltpu.make_async_copy(ybufs[slot], g_ref.at[pl.ds(0, bm * OUT_SLAB)], sem_out.at[slot]).wait()

    @pl.when(b == 0)
    def _():
        for slot in range(RING - 1):
            gather_rows(src_first_refs[slot], slot)
        for slot in range(1, RING):
            ybufs[slot][...] = jnp.zeros(ybufs[slot].shape, jnp.int32)
        for slot in range(1, RING - 1):
            scatter_rows(dst_spare_refs[slot - 1], slot)

    changed = jnp.logical_or(b == 0, be_ref[b] != be_ref[jnp.maximum(b - 1, 0)])

    @pl.when(changed)
    def _():
        wg16[...] = wg_ref[0].astype(BF16)
        wu16[...] = wu_ref[0].astype(BF16)
        wd16[...] = wd_ref[0].astype(BF16)

    def step(cur):
        prv = (cur + RING - 1) % RING
        nxt = (cur + 1) % RING
        wait_gather(cur)
        gather_rows(src_ahead_ref, prv)
        scatter_rows(dst_prev_ref, prv)
        x = _load_row_slabs(xbufs[cur], bm).astype(BF16)
        gate = _dot(x, wg16[...]) + bg_ref[0]
        up = _dot(x, wu16[...]) + bu_ref[0]
        gate = jnp.minimum(gate, SWIGLU_LIMIT)
        up = jnp.clip(up, -SWIGLU_LIMIT, SWIGLU_LIMIT)
        glu = gate * _sigmoid(gate * SWIGLU_ALPHA)
        mid = ((up + 1.0) * glu).astype(BF16)
        _store_row_slabs(ybufs[cur], _pack_bf16_pairs(_dot(mid, wd16[...]) + bd_ref[0]), bm)
        wait_scatter(nxt)

        @pl.when(b == last)
        def _():
            scatter_rows(dst_cur_ref, cur)
            for back in range(RING - 1):
                wait_scatter((cur + RING - back) % RING)
            for ahead in range(1, RING):
                wait_gather((cur + ahead) % RING)

    for slot in range(RING):
        pl.when(b % RING == slot)(functools.partial(step, slot))


def _experts(h2, slot_src, slot_dst, block_e, n_out_rows, wg, bg, wu, bu, wd, bd, bm):
    slab = SUBLANES
    d = wg.shape[1]
    f = wg.shape[2]
    n_blocks = slot_src.shape[0] // bm
    spare = lambda blk: n_out_rows - (RING - blk % RING) * bm + jnp.arange(bm, dtype=jnp.int32)
    src3 = (slot_src * slab).reshape(n_blocks, 1, bm)
    standins = [spare(j - RING) for j in range(1, RING - 1)]
    dst3 = (jnp.concatenate([spare(-1), slot_dst] + standins) * OUT_SLAB).reshape(n_blocks + RING - 1, 1, bm)
    smem_blk = lambda imap: pl.BlockSpec((1, 1, bm), imap, memory_space=pltpu.SMEM)
    fixed_blk = lambda j: smem_blk(lambda b, be: (j, 0, 0))
    wmap = lambda b, be: (be[b], 0, 0)
    grid_spec = pltpu.PrefetchScalarGridSpec(
        num_scalar_prefetch=1,
        grid=(n_blocks,),
        in_specs=[fixed_blk(j) for j in range(RING - 1)] + [
            smem_blk(lambda b, be: (jnp.minimum(b + RING - 1, n_blocks - 1), 0, 0)),
            smem_blk(lambda b, be: (b, 0, 0)),
            smem_blk(lambda b, be: (b + 1, 0, 0)),
        ] + [fixed_blk(n_blocks + j) for j in range(1, RING - 1)] + [
            pl.BlockSpec(memory_space=pl.ANY),
            pl.BlockSpec((1, d, f), wmap),
            pl.BlockSpec((1, 1, f), wmap),
            pl.BlockSpec((1, d, f), wmap),
            pl.BlockSpec((1, 1, f), wmap),
            pl.BlockSpec((1, f, d), wmap),
            pl.BlockSpec((1, 1, d), wmap),
        ],
        out_specs=pl.BlockSpec(memory_space=pl.ANY),
        scratch_shapes=[pltpu.VMEM((bm * slab, LANES), F32)] * RING
        + [pltpu.VMEM((bm * OUT_SLAB, LANES), jnp.int32)] * RING + [
            pltpu.VMEM((d, f), BF16),
            pltpu.VMEM((d, f), BF16),
            pltpu.VMEM((f, d), BF16),
            pltpu.SemaphoreType.DMA((RING,)),
            pltpu.SemaphoreType.DMA((RING,)),
        ],
    )
    return pl.pallas_call(
        _experts_kernel,
        grid_spec=grid_spec,
        out_shape=jax.ShapeDtypeStruct((n_out_rows * OUT_SLAB, LANES), jnp.int32),
        compiler_params=pltpu.CompilerParams(
            dimension_semantics=("arbitrary",), vmem_limit_bytes=VMEM_LIMIT),
        name="experts",
    )(block_e, *([src3] * RING), *([dst3] * RING), h2, wg, bg, wu, bu, wd, bd)


def _combine_kernel(y0_ref, y1_ref, y2_ref, y3_ref, x1_ref, gates_ref, fw_ref, o_ref):
    tm = x1_ref.shape[0]
    acc = x1_ref[...]
    gates = gates_ref[...]
    for k, y_ref in enumerate((y0_ref, y1_ref, y2_ref, y3_ref)):
        y = _unpack_bf16_pairs(_load_row_slabs(y_ref, tm))
        acc = acc + y * gates[:, k:k + 1]
    o_ref[...] = acc * lax.rsqrt(jnp.mean(acc * acc, axis=-1, keepdims=True) + NORM_EPS) * fw_ref[...]


def _combine(ys, x1, gates, final_w, tm):
    t, d = x1.shape
    n_tiles = t // tm
    row = lambda i: (i, 0)
    y_spec = lambda k: pl.BlockSpec((tm * OUT_SLAB, LANES), lambda i: (k * n_tiles + i, 0))
    return pl.pallas_call(
        _combine_kernel,
        grid=(n_tiles,),
        in_specs=[y_spec(k) for k in range(TOP_K)] + [
            pl.BlockSpec((tm, d), row),
            pl.BlockSpec((tm, LANES), row),
            pl.BlockSpec((1, d), lambda i: (0, 0)),
        ],
        out_specs=pl.BlockSpec((tm, d), row),
        out_shape=jax.ShapeDtypeStruct((t, d), F32),
        compiler_params=pltpu.CompilerParams(
            dimension_semantics=("parallel",), vmem_limit_bytes=VMEM_LIMIT),
        name="combine",
    )(ys, ys, ys, ys, x1, gates, final_w)


def _rel_bias_rows(rel_bias):
    m = np.arange(ATT_EXT)
    m = np.where(m < ATT_WIN, m, m - ATT_EXT)
    dist = CA_LEFT_CHUNKS * CHUNK - m
    idx = np.clip(dist, -MAX_REL_DIST, MAX_REL_DIST) + MAX_REL_DIST
    return rel_bias.astype(F32)[:, idx]


def _layer(x, norm1_w, w_in, conv_w, a_log, dt_bias, dn_norm_w, rel_bias, attn_norm_w, w_out,
           norm2_w, w_router, b_router, w_gate, b_gate, w_up, b_up, w_down, b_down, out_norm_w):
    b, s, d = x.shape
    assert d == SUBLANES * LANES, "row-slab layout holds one model row per (8, 128) tile"
    assert s % ATT_ROWS == 0 and s % PREP_ROWS == 0
    t = b * s
    x2 = x.reshape(t, d)

    ba_lo = 4 * DN_WIDTH
    ba_hi = ba_lo + 2 * DN_HEADS
    w_small = jnp.pad(w_in[:, ba_lo:ba_hi], ((0, 0), (0, LANES - 2 * DN_HEADS)))
    w_all = jnp.concatenate([w_in[:, :ba_lo], w_in[:, ba_hi:], w_small], axis=1).astype(BF16)
    qkva, z_a, ba, q_b, k_b, v_b = _in_proj(x2, norm1_w.reshape(1, d), w_all, tm=512)

    lane_pad = (DN_HEADS, LANES - 2 * DN_HEADS)
    alog_row = jnp.pad(a_log.astype(F32), lane_pad).reshape(1, LANES)
    dtb_row = jnp.pad(dt_bias.astype(F32), lane_pad).reshape(1, LANES)
    dn_w, dn_qd, dn_kd, dn_u, dn_qk, dn_cd = _dn_prep(
        qkva.reshape(b, s, 3 * DN_WIDTH), ba.reshape(b, s, LANES), conv_w, alog_row, dtb_row)
    o_a = _dn_scan(dn_w, dn_qd, dn_kd, dn_u, dn_qk, dn_cd, z_a.reshape(b, s, DN_WIDTH),
                   dn_norm_w.reshape(1, DN_HEAD_DIM))
    o_b = _band_attn(q_b.reshape(b, s, CA_WIDTH), k_b.reshape(b, s, CA_WIDTH),
                     v_b.reshape(b, s, CA_WIDTH), _rel_bias_rows(rel_bias),
                     attn_norm_w.reshape(1, CA_WIDTH))

    wr_pad = jnp.pad(w_router.astype(F32), ((0, 0), (0, LANES - N_EXPERTS)))
    wr_hi = wr_pad.astype(BF16)
    br_pad = jnp.pad(b_router.astype(F32), (0, LANES - N_EXPERTS)).reshape(1, LANES)
    x1, h2, route, gates, counts = _out_router(
        o_a.reshape(t, DN_WIDTH), o_b.reshape(t, CA_WIDTH), x2, w_out.astype(BF16),
        norm2_w.reshape(1, d), wr_hi, (wr_pad - wr_hi.astype(F32)).astype(BF16), br_pad, tm=512)

    bm = EXPERT_BLOCK_ROWS
    n_assign = t * TOP_K
    n_blocks = n_assign // bm + N_EXPERTS
    n_slots = n_blocks * bm
    counts_i = counts[0, :N_EXPERTS].astype(jnp.int32)
    padded = (counts_i + bm - 1) // bm * bm
    pad_end = jnp.cumsum(padded)
    pad_start = pad_end - padded
    top_e = route[:, :TOP_K]
    expert_ids = jnp.arange(N_EXPERTS, dtype=jnp.int32)
    start_of = jnp.sum(jnp.where(top_e[..., None] == expert_ids, pad_start, 0), axis=-1)
    pos = start_of + route[:, TOP_K:2 * TOP_K]
    slot_assign = _invert_slots(pos.reshape(-1), n_slots)
    n_out_rows = n_assign + RING * bm
    slot_id = jnp.arange(n_slots, dtype=jnp.int32)
    spare_row = n_assign + (slot_id // bm % RING) * bm + slot_id % bm
    is_real = slot_assign >= 0
    slot_src = jnp.where(is_real, slot_assign // TOP_K, 0)
    slot_dst = jnp.where(is_real, (slot_assign % TOP_K) * t + slot_assign // TOP_K, spare_row)
    block_start = jnp.arange(n_blocks, dtype=jnp.int32) * bm
    block_e = jnp.minimum(jnp.sum((pad_end[None, :] <= block_start[:, None]).astype(jnp.int32), axis=1),
                          N_EXPERTS - 1)

    f = w_gate.shape[-1]
    ys = _experts(h2, slot_src, slot_dst, block_e, n_out_rows,
                  w_gate, b_gate.reshape(N_EXPERTS, 1, f), w_up, b_up.reshape(N_EXPERTS, 1, f),
                  w_down, b_down.reshape(N_EXPERTS, 1, d), bm)
    out = _combine(ys, x1, gates, out_norm_w.reshape(1, d), tm=512)
    return out.reshape(b, s, d)


def kernel(x, norm1_w, w_in, conv_w, a_log, dt_bias, dn_norm_w, rel_bias, attn_norm_w, w_out, norm2_w, w_router, b_router, w_gate, b_gate, w_up, b_up, w_down, b_down, final_norm_w):
    depth = norm1_w.shape[0]
    assert depth == 1, "the final RMSNorm is fused into the last layer's combine step"
    return _layer(x, norm1_w[0], w_in[0], conv_w[0], a_log[0], dt_bias[0], dn_norm_w[0],
                  rel_bias[0], attn_norm_w[0], w_out[0], norm2_w[0], w_router[0], b_router[0],
                  w_gate[0], b_gate[0], w_up[0], b_up[0], w_down[0], b_down[0], final_norm_w)
```

```python
import functools

import jax
import numpy as np
import jax.numpy as jnp
from jax import lax
from jax.experimental import pallas as pl
from jax.experimental.pallas import tpu as pltpu

F32 = jnp.float32
BF16 = jnp.bfloat16
HIGHEST = lax.Precision.HIGHEST

NORM_EPS = 1e-6
CHUNK = 64
DN_HEADS = 4
DN_HEAD_DIM = 128
DN_WIDTH = DN_HEADS * DN_HEAD_DIM
CONV_WIDTH = 4
CA_HEADS = 8
CA_HEAD_DIM = 64
CA_WIDTH = CA_HEADS * CA_HEAD_DIM
CA_LEFT_CHUNKS = 8
MAX_REL_DIST = 256
N_EXPERTS = 32
TOP_K = 4
SWIGLU_LIMIT = 7.0
SWIGLU_ALPHA = 1.702

LANES = 128
SUBLANES = 8
INV_BLOCK = 16
PREP_ROWS = 256
SCAN_GROUP = 32
ATT_ROWS = 256
ATT_WIN = ATT_ROWS + CA_LEFT_CHUNKS * CHUNK
ATT_EXT = 1024
EXPERT_BLOCK_ROWS = 256
OUT_SLAB = 4
RING = 6
INVERT_CHUNK = 8192
INVERT_UNROLL = 8
VMEM_LIMIT = 48 * 1024 * 1024


def _dot(a, b, dims=(((1,), (0,)), ((), ())), precision=None):
    return lax.dot_general(a, b, dims, precision=precision, preferred_element_type=F32)


def _dot_nt(a, b, precision=None):
    return _dot(a, b, (((1,), (1,)), ((), ())), precision)


def _bdot(a, b):
    return _dot(a.astype(BF16), b.astype(BF16))


def _dot_split(a, b, b_hi):
    a_hi = a.astype(BF16)
    a_lo = (a - a_hi.astype(F32)).astype(BF16)
    b_lo = (b - b_hi.astype(F32)).astype(BF16)
    return _dot(a_hi, b_hi) + (_dot(a_hi, b_lo) + _dot(a_lo, b_hi))


def _store_row_slabs(ref, value, rows):
    n = value.shape[1] // LANES
    for j in range(n):
        ref[pl.ds(j, rows, stride=n), :] = value[:, j * LANES:(j + 1) * LANES]


def _load_row_slabs(ref, rows):
    n = ref.shape[0] // rows
    return jnp.concatenate([ref[pl.ds(j, rows, stride=n), :] for j in range(n)], axis=-1)


def _pack_bf16_pairs(x):
    blocks = [x[:, j:j + LANES] for j in range(0, x.shape[1], LANES)]
    pairs = [pltpu.pack_elementwise([blocks[j], blocks[j + 1]], packed_dtype=BF16)
             for j in range(0, len(blocks), 2)]
    return jnp.concatenate(pairs, axis=-1)


def _unpack_bf16_pairs(w):
    blocks = []
    for j in range(0, w.shape[1], LANES):
        for i in range(2):
            blocks.append(pltpu.unpack_elementwise(w[:, j:j + LANES], index=i, packed_dtype=BF16,
                                                   unpacked_dtype=F32))
    return jnp.concatenate(blocks, axis=-1)


def _sigmoid(x):
    return 1.0 / (1.0 + jnp.exp(-x))


def _silu(x):
    return x * _sigmoid(x)


def _in_proj_kernel(x_ref, nw_ref, w_ref, qkva_ref, z_ref, ba_ref, qb_ref, kb_ref, vb_ref):
    x = x_ref[...]
    h = x * lax.rsqrt(jnp.mean(x * x, axis=-1, keepdims=True) + NORM_EPS) * nw_ref[...]
    hb = h.astype(BF16)
    c0 = 3 * DN_WIDTH
    c1 = c0 + DN_WIDTH
    c2 = c1 + CA_WIDTH
    c3 = c2 + CA_WIDTH
    c4 = c3 + CA_WIDTH
    qkva_ref[...] = _dot(hb, w_ref[:, 0:c0])
    z_ref[...] = _dot(hb, w_ref[:, c0:c1])
    qb_ref[...] = _dot(hb, w_ref[:, c1:c2]).astype(BF16)
    kb_ref[...] = _dot(hb, w_ref[:, c2:c3]).astype(BF16)
    vb_ref[...] = _dot(hb, w_ref[:, c3:c4]).astype(BF16)
    ba_ref[...] = _dot(hb, w_ref[:, c4:c4 + LANES])


def _in_proj(x2, norm_w, w_all, tm):
    t, d = x2.shape
    wcols = w_all.shape[1]
    row = lambda i: (i, 0)
    fixed = lambda i: (0, 0)
    return pl.pallas_call(
        _in_proj_kernel,
        grid=(t // tm,),
        in_specs=[
            pl.BlockSpec((tm, d), row),
            pl.BlockSpec((1, d), fixed),
            pl.BlockSpec((d, wcols), fixed),
        ],
        out_specs=[
            pl.BlockSpec((tm, 3 * DN_WIDTH), row),
            pl.BlockSpec((tm, DN_WIDTH), row),
            pl.BlockSpec((tm, LANES), row),
            pl.BlockSpec((tm, CA_WIDTH), row),
            pl.BlockSpec((tm, CA_WIDTH), row),
            pl.BlockSpec((tm, CA_WIDTH), row),
        ],
        out_shape=[
            jax.ShapeDtypeStruct((t, 3 * DN_WIDTH), F32),
            jax.ShapeDtypeStruct((t, DN_WIDTH), F32),
            jax.ShapeDtypeStruct((t, LANES), F32),
            jax.ShapeDtypeStruct((t, CA_WIDTH), BF16),
            jax.ShapeDtypeStruct((t, CA_WIDTH), BF16),
            jax.ShapeDtypeStruct((t, CA_WIDTH), BF16),
        ],
        compiler_params=pltpu.CompilerParams(
            dimension_semantics=("parallel",), vmem_limit_bytes=VMEM_LIMIT),
        name="in_proj",
    )(x2, norm_w, w_all)


def _unit_lower_inverses(lowers, eye, inv_block):
    eye16 = eye.astype(BF16)
    bf = lambda a, b: _dot(a, b).astype(BF16)
    each = lambda fn, *lists: [fn(*args) for args in zip(*lists)]
    diag = each(lambda l: jnp.where(inv_block, l, 0.0), lowers)
    diag16 = each(lambda x: x.astype(BF16), diag)
    off16 = each(lambda l, x: (l - x).astype(BF16), lowers, diag)
    d2 = each(lambda x: bf(x, x), diag16)
    d4 = each(lambda x: bf(x, x), d2)
    d8 = each(lambda x: bf(x, x), d4)
    pa = each(lambda x, y: bf(eye16 - x, eye16 + y), diag16, d2)
    pb = each(lambda x, y: bf(eye16 + x, eye16 + y), d4, d8)
    p = each(bf, pa, pb)
    m = each(bf, p, off16)
    m2 = each(lambda x: bf(x, x), m)
    mm = each(lambda x, y: bf(eye16 - x, eye16 + y), m, m2)
    t0 = each(_dot, mm, p)
    t0_hi = each(lambda x: x.astype(BF16), t0)
    resid = each(lambda l, x, x_hi: (eye - x) - _dot_split(l, x, x_hi), lowers, t0, t0_hi)
    return each(lambda x, x_hi, res: x + _dot(x_hi, res.astype(BF16)), t0, t0_hi, resid)


def _dn_prep_kernel(cur_ref, halo_ref, ba_ref, convw_ref, alog_ref, dtb_ref,
                    w_ref, qd_ref, kd_ref, u_ref, qk_ref, cd_ref):
    n = pl.program_id(1)
    r = PREP_ROWS
    cur = cur_ref[0]
    halo = jnp.where(n > 0, halo_ref[0], 0.0)
    full = jnp.concatenate([halo, cur], axis=0)
    conv = full[SUBLANES:] * convw_ref[CONV_WIDTH - 1:CONV_WIDTH, :]
    for j in range(CONV_WIDTH - 1):
        shift = CONV_WIDTH - 1 - j
        conv = conv + pltpu.roll(full, shift, axis=0)[SUBLANES:] * convw_ref[j:j + 1, :]
    qkv = _silu(conv)

    ba = ba_ref[0]
    beta_all = _sigmoid(ba)
    sp_in = ba + dtb_ref[...]
    softplus = jnp.maximum(sp_in, 0.0) + jnp.log1p(jnp.exp(-jnp.abs(sp_in)))
    g_all = -jnp.exp(alog_ref[...]) * softplus
    rows = lax.broadcasted_iota(jnp.int32, (r, r), 0)
    cols = lax.broadcasted_iota(jnp.int32, (r, r), 1)
    same_chunk = (rows // CHUNK) == (cols // CHUNK)
    causal = same_chunk & (rows >= cols)
    strict = same_chunk & (rows > cols)
    inv_block = (rows // INV_BLOCK) == (cols // INV_BLOCK)
    eye = (rows == cols).astype(F32)
    gc_all = _dot(causal.astype(F32), g_all, precision=HIGHEST)
    gc_last_all = _dot(same_chunk.astype(F32), g_all, precision=HIGHEST)
    gc_rows = gc_all.T
    chunk_decay = jnp.exp(gc_last_all)
    cd_ref[0] = jnp.concatenate(
        [chunk_decay[c * CHUNK:c * CHUNK + SUBLANES] for c in range(r // CHUNK)], axis=0)

    scale = DN_HEAD_DIM ** -0.5
    lowers, rhss, qk_parts = [], [], []
    for h in range(DN_HEADS):
        lo = h * DN_HEAD_DIM
        q = qkv[:, lo:lo + DN_HEAD_DIM]
        k = qkv[:, DN_WIDTH + lo:DN_WIDTH + lo + DN_HEAD_DIM]
        v = qkv[:, 2 * DN_WIDTH + lo:2 * DN_WIDTH + lo + DN_HEAD_DIM]
        q = q * lax.rsqrt(jnp.sum(q * q, axis=-1, keepdims=True) + NORM_EPS) * scale
        k = k * lax.rsqrt(jnp.sum(k * k, axis=-1, keepdims=True) + NORM_EPS)
        beta = beta_all[:, h:h + 1]
        gc = gc_all[:, DN_HEADS + h:DN_HEADS + h + 1]
        gc_row = gc_rows[DN_HEADS + h:DN_HEADS + h + 1, :]
        gc_last = gc_last_all[:, DN_HEADS + h:DN_HEADS + h + 1]
        decay = jnp.where(causal, jnp.exp(jnp.where(causal, gc - gc_row, 0.0)), 0.0)
        egc = jnp.exp(gc)
        k_beta = k * beta
        k16 = k.astype(BF16)
        lowers.append(jnp.where(strict, _dot_nt(k_beta.astype(BF16), k16) * decay, 0.0))
        rhss.append(jnp.concatenate([v * beta, k_beta * egc], axis=-1).astype(BF16))
        qd_ref[0, :, lo:lo + DN_HEAD_DIM] = (q * egc).astype(BF16)
        kd_ref[0, :, lo:lo + DN_HEAD_DIM] = k * jnp.exp(gc_last - gc)
        qk = _dot_nt(q.astype(BF16), k16) * decay
        compact = qk[:, 0:CHUNK]
        for c in range(1, r // CHUNK):
            compact = compact + qk[:, c * CHUNK:(c + 1) * CHUNK]
        qk_parts.append(compact.astype(BF16))
    qk_ref[0] = jnp.concatenate(qk_parts, axis=-1)

    t_invs = _unit_lower_inverses(lowers, eye, inv_block)
    for h in range(DN_HEADS):
        lo = h * DN_HEAD_DIM
        sol = _dot(t_invs[h].astype(BF16), rhss[h])
        u_ref[0, :, lo:lo + DN_HEAD_DIM] = sol[:, :DN_HEAD_DIM]
        w_ref[0, :, lo:lo + DN_HEAD_DIM] = sol[:, DN_HEAD_DIM:].astype(BF16)


def _dn_prep(qkva, ba, conv_w, alog_row, dtb_row):
    b, s, _ = qkva.shape
    r = PREP_ROWS
    n_chunks = s // CHUNK
    halo_blocks = r // SUBLANES
    fixed = lambda i, n: (0, 0)
    blk = lambda i, n: (i, n, 0)
    return pl.pallas_call(
        _dn_prep_kernel,
        grid=(b, s // r),
        in_specs=[
            pl.BlockSpec((1, r, 3 * DN_WIDTH), blk),
            pl.BlockSpec((1, SUBLANES, 3 * DN_WIDTH),
                         lambda i, n: (i, jnp.maximum(n * halo_blocks - 1, 0), 0)),
            pl.BlockSpec((1, r, LANES), blk),
            pl.BlockSpec((CONV_WIDTH, 3 * DN_WIDTH), fixed),
            pl.BlockSpec((1, LANES), fixed),
            pl.BlockSpec((1, LANES), fixed),
        ],
        out_specs=[
            pl.BlockSpec((1, r, DN_WIDTH), blk),
            pl.BlockSpec((1, r, DN_WIDTH), blk),
            pl.BlockSpec((1, r, DN_WIDTH), blk),
            pl.BlockSpec((1, r, DN_WIDTH), blk),
            pl.BlockSpec((1, r, DN_HEADS * CHUNK), blk),
            pl.BlockSpec((1, r // CHUNK * SUBLANES, LANES), blk),
        ],
        out_shape=[
            jax.ShapeDtypeStruct((b, s, DN_WIDTH), BF16),
            jax.ShapeDtypeStruct((b, s, DN_WIDTH), BF16),
            jax.ShapeDtypeStruct((b, s, DN_WIDTH), F32),
            jax.ShapeDtypeStruct((b, s, DN_WIDTH), F32),
            jax.ShapeDtypeStruct((b, s, DN_HEADS * CHUNK), BF16),
            jax.ShapeDtypeStruct((b, n_chunks * SUBLANES, LANES), F32),
        ],
        compiler_params=pltpu.CompilerParams(
            dimension_semantics=("parallel", "parallel"), vmem_limit_bytes=VMEM_LIMIT),
        name="dn_prep",
    )(qkva, qkva, ba, conv_w, alog_row, dtb_row)


def _dn_scan_kernel(w_ref, qd_ref, kd_ref, u_ref, qk_ref, cd_ref, z_ref, normw_ref, o_ref, state_ref):
    n = pl.program_id(0)

    @pl.when(n == 0)
    def _():
        state_ref[...] = jnp.zeros_like(state_ref)

    n_batch = w_ref.shape[0]
    chains = [(bi, h) for bi in range(n_batch) for h in range(DN_HEADS)]
    for g in range(0, len(chains), SCAN_GROUP):
        group = chains[g:g + SCAN_GROUP]
        col = lambda h: slice(h * DN_HEAD_DIM, (h + 1) * DN_HEAD_DIM)
        s16 = [state_ref[bi * DN_HEADS + h].astype(BF16) for bi, h in group]
        ws = [_dot(w_ref[bi, :, col(h)], s) for (bi, h), s in zip(group, s16)]
        qs = [_dot(qd_ref[bi, :, col(h)], s) for (bi, h), s in zip(group, s16)]
        vn16 = [(u_ref[bi, :, col(h)] - x).astype(BF16) for (bi, h), x in zip(group, ws)]
        inner = [_dot(qk_ref[bi, :, h * CHUNK:(h + 1) * CHUNK], v) for (bi, h), v in zip(group, vn16)]
        kd_t = [kd_ref[bi, :, col(h)].T.astype(BF16) for bi, h in group]
        upd = [_dot(k, v) for k, v in zip(kd_t, vn16)]
        for (bi, h), x in zip(group, upd):
            cd = cd_ref[bi, 0:1, DN_HEADS + h:DN_HEADS + h + 1]
            state_ref[bi * DN_HEADS + h] = state_ref[bi * DN_HEADS + h] * cd + x
        for (bi, h), a, c in zip(group, qs, inner):
            o = a + c
            o = o * lax.rsqrt(jnp.mean(o * o, axis=-1, keepdims=True) + NORM_EPS) * normw_ref[...]
            o_ref[bi, :, col(h)] = o * _silu(z_ref[bi, :, col(h)])


def _dn_scan(w, qd, kd, u, qk, cd, z, norm_w):
    b, s, _ = w.shape
    n_chunks = s // CHUNK
    blk = lambda n: (0, n, 0)
    wide = pl.BlockSpec((b, CHUNK, DN_WIDTH), blk)
    return pl.pallas_call(
        _dn_scan_kernel,
        grid=(n_chunks,),
        in_specs=[
            wide, wide, wide, wide,
            pl.BlockSpec((b, CHUNK, DN_HEADS * CHUNK), blk),
            pl.BlockSpec((b, SUBLANES, LANES), blk),
            wide,
            pl.BlockSpec((1, DN_HEAD_DIM), lambda n: (0, 0)),
        ],
        out_specs=wide,
        out_shape=jax.ShapeDtypeStruct((b, s, DN_WIDTH), F32),
        scratch_shapes=[pltpu.VMEM((b * DN_HEADS, DN_HEAD_DIM, DN_HEAD_DIM), F32)],
        compiler_params=pltpu.CompilerParams(
            dimension_semantics=("arbitrary",), vmem_limit_bytes=VMEM_LIMIT),
        name="dn_scan",
    )(w, qd, kd, u, qk, cd, z, norm_w)


def _band_attn_kernel(q_ref, k_ref, v_ref, ext_ref, nw_ref, o_ref, bias_ref):
    first = jnp.logical_and(pl.program_id(0) == 0, pl.program_id(1) == 0)

    @pl.when(first)
    def _():
        qi = lax.broadcasted_iota(jnp.int32, (ATT_ROWS, ATT_WIN), 0) // CHUNK
        kj = lax.broadcasted_iota(jnp.int32, (ATT_ROWS, ATT_WIN), 1) // CHUNK
        in_band = jnp.logical_and(kj >= qi, kj <= qi + CA_LEFT_CHUNKS)
        for h in range(CA_HEADS):
            tiled = jnp.broadcast_to(ext_ref[h:h + 1, :], (ATT_ROWS, ATT_EXT))
            toeplitz = pltpu.roll(tiled, 0, axis=1, stride=1, stride_axis=0)
            bias_ref[h] = jnp.where(in_band, toeplitz[:, :ATT_WIN], -jnp.inf)

    n0 = pl.program_id(1) * (ATT_ROWS // CHUNK)
    k_parts, v_parts = [], []
    for w in range(ATT_WIN // CHUNK):
        start = pl.multiple_of(jnp.maximum(n0 - CA_LEFT_CHUNKS + w, 0) * CHUNK, CHUNK)
        k_parts.append(k_ref[0, pl.ds(start, CHUNK), :])
        v_parts.append(v_ref[0, pl.ds(start, CHUNK), :])
    k_win = jnp.concatenate(k_parts, axis=0)
    v_win = jnp.concatenate(v_parts, axis=0)
    q = q_ref[0] * jnp.asarray(CA_HEAD_DIM ** -0.5, BF16)

    key_col = lax.broadcasted_iota(jnp.int32, (1, ATT_WIN), 1)
    before_start = key_col < (CA_LEFT_CHUNKS - n0) * CHUNK
    col_bias = jnp.where(before_start, -jnp.inf, 0.0)
    lane = lax.broadcasted_iota(jnp.int32, (ATT_ROWS, LANES), 1)
    heads_per_group = LANES // CA_HEAD_DIM

    outs = []
    for grp in range(CA_WIDTH // LANES):
        lo = grp * LANES
        qg = q[:, lo:lo + LANES]
        kg = k_win[:, lo:lo + LANES]
        vg = v_win[:, lo:lo + LANES]
        og = jnp.zeros((ATT_ROWS, LANES), F32)
        in_heads = [(lane // CA_HEAD_DIM) == hh for hh in range(heads_per_group)]
        scores = [_dot_nt(jnp.where(m, qg, jnp.zeros_like(qg)), kg) for m in in_heads]
        weights, denoms = [], []
        for hh, s in enumerate(scores):
            s = s + bias_ref[grp * heads_per_group + hh] + col_bias
            e = jnp.exp(s - jnp.max(s, axis=-1, keepdims=True))
            denoms.append(jnp.sum(e, axis=-1, keepdims=True))
            weights.append(e.astype(BF16))
        for m, e, denom in zip(in_heads, weights, denoms):
            og = jnp.where(m, _dot(e, vg) / denom, og)
        outs.append(og)
    o = jnp.concatenate(outs, axis=-1)
    o = o * lax.rsqrt(jnp.mean(o * o, axis=-1, keepdims=True) + NORM_EPS) * nw_ref[...]
    o_ref[0] = o


def _band_attn(qb, kb, vb, ext, norm_w):
    b, s, _ = qb.shape
    return pl.pallas_call(
        _band_attn_kernel,
        grid=(b, s // ATT_ROWS),
        in_specs=[
            pl.BlockSpec((1, ATT_ROWS, CA_WIDTH), lambda i, n: (i, n, 0)),
            pl.BlockSpec((1, s, CA_WIDTH), lambda i, n: (i, 0, 0)),
            pl.BlockSpec((1, s, CA_WIDTH), lambda i, n: (i, 0, 0)),
            pl.BlockSpec((CA_HEADS, ATT_EXT), lambda i, n: (0, 0)),
            pl.BlockSpec((1, CA_WIDTH), lambda i, n: (0, 0)),
        ],
        out_specs=pl.BlockSpec((1, ATT_ROWS, CA_WIDTH), lambda i, n: (i, n, 0)),
        out_shape=jax.ShapeDtypeStruct((b, s, CA_WIDTH), F32),
        scratch_shapes=[pltpu.VMEM((CA_HEADS, ATT_ROWS, ATT_WIN), F32)],
        compiler_params=pltpu.CompilerParams(
            dimension_semantics=("arbitrary", "arbitrary"), vmem_limit_bytes=VMEM_LIMIT),
        name="band_attn",
    )(qb, kb, vb, ext, norm_w)


def _out_router_kernel(oa_ref, ob_ref, x_ref, wout_ref, n2_ref, wr_hi_ref, wr_lo_ref, br_ref,
                       x1_ref, h2_ref, route_ref, gates_ref, counts_ref, run_ref, earlier_ref):
    i = pl.program_id(0)
    tm = x_ref.shape[0]

    @pl.when(i == 0)
    def _():
        run_ref[...] = jnp.zeros_like(run_ref)
        r = lax.broadcasted_iota(jnp.int32, (tm, tm), 0)
        c = lax.broadcasted_iota(jnp.int32, (tm, tm), 1)
        earlier_ref[...] = (r > c).astype(BF16)

    x1 = (x_ref[...]
          + _dot(oa_ref[...].astype(BF16), wout_ref[0:DN_WIDTH, :])
          + _dot(ob_ref[...].astype(BF16), wout_ref[DN_WIDTH:, :]))
    x1_ref[...] = x1
    h2 = x1 * lax.rsqrt(jnp.mean(x1 * x1, axis=-1, keepdims=True) + NORM_EPS) * n2_ref[...]
    _store_row_slabs(h2_ref, h2, tm)
    lane = lax.broadcasted_iota(jnp.int32, (tm, LANES), 1)
    h2_hi = h2.astype(BF16)
    h2_lo = (h2 - h2_hi.astype(F32)).astype(BF16)
    logits = (_dot(h2_hi, wr_hi_ref[...]) + (_dot(h2_hi, wr_lo_ref[...]) + _dot(h2_lo, wr_hi_ref[...]))
              + br_ref[...])
    logits = jnp.where(lane < N_EXPERTS, logits, -jnp.inf)

    top_vals, top_idx, onehots = [], [], []
    for _ in range(TOP_K):
        m = jnp.max(logits, axis=-1, keepdims=True)
        idx = jnp.min(jnp.where(logits == m, lane, LANES), axis=-1, keepdims=True)
        hit = lane == idx
        top_vals.append(m)
        top_idx.append(idx)
        onehots.append(hit)
        logits = jnp.where(hit, -jnp.inf, logits)

    exps = [jnp.exp(v - top_vals[0]) for v in top_vals]
    denom = exps[0] + exps[1] + exps[2] + exps[3]
    gates = jnp.zeros((tm, LANES), F32)
    for k in range(TOP_K):
        gates = jnp.where(lane == k, exps[k] / denom, gates)
    gates_ref[...] = gates

    multi = jnp.zeros((tm, LANES), F32)
    for hit in onehots:
        multi = multi + hit.astype(F32)
    before = _dot(earlier_ref[...], multi.astype(BF16)) + run_ref[...]
    route = jnp.zeros((tm, LANES), jnp.int32)
    for k in range(TOP_K):
        rank = jnp.sum(jnp.where(onehots[k], before, 0.0), axis=-1, keepdims=True).astype(jnp.int32)
        route = jnp.where(lane == k, top_idx[k], route)
        route = jnp.where(lane == TOP_K + k, rank, route)
    route_ref[...] = route
    run_ref[...] = run_ref[...] + jnp.sum(multi, axis=0, keepdims=True)
    counts_ref[...] = run_ref[...]


def _out_router(oa, ob, x2, w_out, norm2_w, wr_hi, wr_lo, br_pad, tm):
    t, d = x2.shape
    row = lambda i: (i, 0)
    fixed = lambda i: (0, 0)
    return pl.pallas_call(
        _out_router_kernel,
        grid=(t // tm,),
        in_specs=[
            pl.BlockSpec((tm, DN_WIDTH), row),
            pl.BlockSpec((tm, CA_WIDTH), row),
            pl.BlockSpec((tm, d), row),
            pl.BlockSpec((DN_WIDTH + CA_WIDTH, d), fixed),
            pl.BlockSpec((1, d), fixed),
            pl.BlockSpec((d, LANES), fixed),
            pl.BlockSpec((d, LANES), fixed),
            pl.BlockSpec((1, LANES), fixed),
        ],
        out_specs=[
            pl.BlockSpec((tm, d), row),
            pl.BlockSpec((tm * SUBLANES, LANES), row),
            pl.BlockSpec((tm, LANES), row),
            pl.BlockSpec((tm, LANES), row),
            pl.BlockSpec((1, LANES), fixed),
        ],
        out_shape=[
            jax.ShapeDtypeStruct((t, d), F32),
            jax.ShapeDtypeStruct((t * SUBLANES, LANES), F32),
            jax.ShapeDtypeStruct((t, LANES), jnp.int32),
            jax.ShapeDtypeStruct((t, LANES), F32),
            jax.ShapeDtypeStruct((1, LANES), F32),
        ],
        scratch_shapes=[pltpu.VMEM((1, LANES), F32), pltpu.VMEM((tm, tm), BF16)],
        compiler_params=pltpu.CompilerParams(
            dimension_semantics=("arbitrary",), vmem_limit_bytes=VMEM_LIMIT),
        name="out_router",
    )(oa, ob, x2, w_out, norm2_w, wr_hi, wr_lo, br_pad)


def _invert_slots_kernel(pos_ref, fill_ref, o_ref, sem):
    i = pl.program_id(0)
    chunk = pos_ref.shape[2]

    @pl.when(i == 0)
    def _():
        fill = pltpu.make_async_copy(fill_ref, o_ref, sem)
        fill.start()
        fill.wait()

    base = i * chunk

    def place(a, carry):
        o_ref[pos_ref[0, 0, a]] = base + a
        return carry

    lax.fori_loop(0, chunk, place, 0, unroll=INVERT_UNROLL)


def _invert_slots(pos, n_slots):
    n = pos.shape[0]
    chunk = INVERT_CHUNK
    return pl.pallas_call(
        _invert_slots_kernel,
        grid=(n // chunk,),
        in_specs=[pl.BlockSpec((1, 1, chunk), lambda i: (i, 0, 0), memory_space=pltpu.SMEM),
                  pl.BlockSpec(memory_space=pl.ANY)],
        out_specs=pl.BlockSpec((n_slots,), lambda i: (0,), memory_space=pltpu.SMEM),
        out_shape=jax.ShapeDtypeStruct((n_slots,), jnp.int32),
        scratch_shapes=[pltpu.SemaphoreType.DMA],
        compiler_params=pltpu.CompilerParams(dimension_semantics=("arbitrary",)),
        name="invert_slots",
    )(pos.reshape(n // chunk, 1, chunk), jnp.full((n_slots,), -1, jnp.int32))


def _experts_kernel(be_ref, *refs):
    refs = list(refs)
    take = lambda n: [refs.pop(0) for _ in range(n)]
    src_first_refs = take(RING - 1)
    src_ahead_ref, dst_prev_ref, dst_cur_ref = take(3)
    dst_spare_refs = take(RING - 2)
    h2_ref, wg_ref, bg_ref, wu_ref, bu_ref, wd_ref, bd_ref, g_ref = take(8)
    xbufs = take(RING)
    ybufs = take(RING)
    wg16, wu16, wd16, sem_in, sem_out = refs
    b = pl.program_id(0)
    last = pl.num_programs(0) - 1
    slab = SUBLANES
    bm = xbufs[0].shape[0] // slab

    def gather_rows(idx_ref, slot):
        for r in range(bm):
            src = pl.multiple_of(idx_ref[0, 0, r], slab)
            pltpu.make_async_copy(h2_ref.at[pl.ds(src, slab)],
                                  xbufs[slot].at[pl.ds(r * slab, slab)], sem_in.at[slot]).start()

    def scatter_rows(idx_ref, slot):
        for r in range(bm):
            dst = pl.multiple_of(idx_ref[0, 0, r], OUT_SLAB)
            pltpu.make_async_copy(ybufs[slot].at[pl.ds(r * OUT_SLAB, OUT_SLAB)],
                                  g_ref.at[pl.ds(dst, OUT_SLAB)], sem_out.at[slot]).start()

    def wait_gather(slot):
        pltpu.make_async_copy(h2_ref.at[pl.ds(0, bm * slab)], xbufs[slot], sem_in.at[slot]).wait()

    def wait_scatter(slot):
        pltpu.make_async_copy(ybufs[slot], g_ref.at[pl.ds(0, bm * OUT_SLAB)], sem_out.at[slot]).wait()

    @pl.when(b == 0)
    def _():
        for slot in range(RING - 1):
            gather_rows(src_first_refs[slot], slot)
        for slot in range(1, RING):
            ybufs[slot][...] = jnp.zeros(ybufs[slot].shape, jnp.int32)
        for slot in range(1, RING - 1):
            scatter_rows(dst_spare_refs[slot - 1], slot)

    changed = jnp.logical_or(b == 0, be_ref[b] != be_ref[jnp.maximum(b - 1, 0)])

    @pl.when(changed)
    def _():
        wg16[...] = wg_ref[0].astype(BF16)
        wu16[...] = wu_ref[0].astype(BF16)
        wd16[...] = wd_ref[0].astype(BF16)

    def step(cur):
        prv = (cur + RING - 1) % RING
        nxt = (cur + 1) % RING
        wait_gather(cur)
        gather_rows(src_ahead_ref, prv)
        scatter_rows(dst_prev_ref, prv)
        x = _load_row_slabs(xbufs[cur], bm).astype(BF16)
        gate = _dot(x, wg16[...]) + bg_ref[0]
        up = _dot(x, wu16[...]) + bu_ref[0]
        gate = jnp.minimum(gate, SWIGLU_LIMIT)
        up = jnp.clip(up, -SWIGLU_LIMIT, SWIGLU_LIMIT)
        glu = gate * _sigmoid(gate * SWIGLU_ALPHA)
        mid = ((up + 1.0) * glu).astype(BF16)
        _store_row_slabs(ybufs[cur], _pack_bf16_pairs(_dot(mid, wd16[...]) + bd_ref[0]), bm)
        wait_scatter(nxt)

        @pl.when(b == last)
        def _():
            scatter_rows(dst_cur_ref, cur)
            for back in range(RING - 1):
                wait_scatter((cur + RING - back) % RING)
            for ahead in range(1, RING):
                wait_gather((cur + ahead) % RING)

    for slot in range(RING):
        pl.when(b % RING == slot)(functools.partial(step, slot))


def _experts(h2, slot_src, slot_dst, block_e, n_out_rows, wg, bg, wu, bu, wd, bd, bm):
    slab = SUBLANES
    d = wg.shape[1]
    f = wg.shape[2]
    n_blocks = slot_src.shape[0] // bm
    spare = lambda blk: n_out_rows - (RING - blk % RING) * bm + jnp.arange(bm, dtype=jnp.int32)
    src3 = (slot_src * slab).reshape(n_blocks, 1, bm)
    standins = [spare(j - RING) for j in range(1, RING - 1)]
    dst3 = (jnp.concatenate([spare(-1), slot_dst] + standins) * OUT_SLAB).reshape(n_blocks + RING - 1, 1, bm)
    smem_blk = lambda imap: pl.BlockSpec((1, 1, bm), imap, memory_space=pltpu.SMEM)
    fixed_blk = lambda j: smem_blk(lambda b, be: (j, 0, 0))
    wmap = lambda b, be: (be[b], 0, 0)
    grid_spec = pltpu.PrefetchScalarGridSpec(
        num_scalar_prefetch=1,
        grid=(n_blocks,),
        in_specs=[fixed_blk(j) for j in range(RING - 1)] + [
            smem_blk(lambda b, be: (jnp.minimum(b + RING - 1, n_blocks - 1), 0, 0)),
            smem_blk(lambda b, be: (b, 0, 0)),
            smem_blk(lambda b, be: (b + 1, 0, 0)),
        ] + [fixed_blk(n_blocks + j) for j in range(1, RING - 1)] + [
            pl.BlockSpec(memory_space=pl.ANY),
            pl.BlockSpec((1, d, f), wmap),
            pl.BlockSpec((1, 1, f), wmap),
            pl.BlockSpec((1, d, f), wmap),
            pl.BlockSpec((1, 1, f), wmap),
            pl.BlockSpec((1, f, d), wmap),
            pl.BlockSpec((1, 1, d), wmap),
        ],
        out_specs=pl.BlockSpec(memory_space=pl.ANY),
        scratch_shapes=[pltpu.VMEM((bm * slab, LANES), F32)] * RING
        + [pltpu.VMEM((bm * OUT_SLAB, LANES), jnp.int32)] * RING + [
            pltpu.VMEM((d, f), BF16),
            pltpu.VMEM((d, f), BF16),
            pltpu.VMEM((f, d), BF16),
            pltpu.SemaphoreType.DMA((RING,)),
            pltpu.SemaphoreType.DMA((RING,)),
        ],
    )
    return pl.pallas_call(
        _experts_kernel,
        grid_spec=grid_spec,
        out_shape=jax.ShapeDtypeStruct((n_out_rows * OUT_SLAB, LANES), jnp.int32),
        compiler_params=pltpu.CompilerParams(
            dimension_semantics=("arbitrary",), vmem_limit_bytes=VMEM_LIMIT),
        name="experts",
    )(block_e, *([src3] * RING), *([dst3] * RING), h2, wg, bg, wu, bu, wd, bd)


def _combine_kernel(y0_ref, y1_ref, y2_ref, y3_ref, x1_ref, gates_ref, fw_ref, o_ref):
    tm = x1_ref.shape[0]
    acc = x1_ref[...]
    gates = gates_ref[...]
    for k, y_ref in enumerate((y0_ref, y1_ref, y2_ref, y3_ref)):
        y = _unpack_bf16_pairs(_load_row_slabs(y_ref, tm))
        acc = acc + y * gates[:, k:k + 1]
    o_ref[...] = acc * lax.rsqrt(jnp.mean(acc * acc, axis=-1, keepdims=True) + NORM_EPS) * fw_ref[...]


def _combine(ys, x1, gates, final_w, tm):
    t, d = x1.shape
    n_tiles = t // tm
    row = lambda i: (i, 0)
    y_spec = lambda k: pl.BlockSpec((tm * OUT_SLAB, LANES), lambda i: (k * n_tiles + i, 0))
    return pl.pallas_call(
        _combine_kernel,
        grid=(n_tiles,),
        in_specs=[y_spec(k) for k in range(TOP_K)] + [
            pl.BlockSpec((tm, d), row),
            pl.BlockSpec((tm, LANES), row),
            pl.BlockSpec((1, d), lambda i: (0, 0)),
        ],
        out_specs=pl.BlockSpec((tm, d), row),
        out_shape=jax.ShapeDtypeStruct((t, d), F32),
        compiler_params=pltpu.CompilerParams(
            dimension_semantics=("parallel",), vmem_limit_bytes=VMEM_LIMIT),
        name="combine",
    )(ys, ys, ys, ys, x1, gates, final_w)


def _rel_bias_rows(rel_bias):
    m = np.arange(ATT_EXT)
    m = np.where(m < ATT_WIN, m, m - ATT_EXT)
    dist = CA_LEFT_CHUNKS * CHUNK - m
    idx = np.clip(dist, -MAX_REL_DIST, MAX_REL_DIST) + MAX_REL_DIST
    return rel_bias.astype(F32)[:, idx]


def _layer(x, norm1_w, w_in, conv_w, a_log, dt_bias, dn_norm_w, rel_bias, attn_norm_w, w_out,
           norm2_w, w_router, b_router, w_gate, b_gate, w_up, b_up, w_down, b_down, out_norm_w):
    b, s, d = x.shape
    assert d == SUBLANES * LANES, "row-slab layout holds one model row per (8, 128) tile"
    assert s % ATT_ROWS == 0 and s % PREP_ROWS == 0
    t = b * s
    x2 = x.reshape(t, d)

    ba_lo = 4 * DN_WIDTH
    ba_hi = ba_lo + 2 * DN_HEADS
    w_small = jnp.pad(w_in[:, ba_lo:ba_hi], ((0, 0), (0, LANES - 2 * DN_HEADS)))
    w_all = jnp.concatenate([w_in[:, :ba_lo], w_in[:, ba_hi:], w_small], axis=1).astype(BF16)
    qkva, z_a, ba, q_b, k_b, v_b = _in_proj(x2, norm1_w.reshape(1, d), w_all, tm=512)

    lane_pad = (DN_HEADS, LANES - 2 * DN_HEADS)
    alog_row = jnp.pad(a_log.astype(F32), lane_pad).reshape(1, LANES)
    dtb_row = jnp.pad(dt_bias.astype(F32), lane_pad).reshape(1, LANES)
    dn_w, dn_qd, dn_kd, dn_u, dn_qk, dn_cd = _dn_prep(
        qkva.reshape(b, s, 3 * DN_WIDTH), ba.reshape(b, s, LANES), conv_w, alog_row, dtb_row)
    o_a = _dn_scan(dn_w, dn_qd, dn_kd, dn_u, dn_qk, dn_cd, z_a.reshape(b, s, DN_WIDTH),
                   dn_norm_w.reshape(1, DN_HEAD_DIM))
    o_b = _band_attn(q_b.reshape(b, s, CA_WIDTH), k_b.reshape(b, s, CA_WIDTH),
                     v_b.reshape(b, s, CA_WIDTH), _rel_bias_rows(rel_bias),
                     attn_norm_w.reshape(1, CA_WIDTH))

    wr_pad = jnp.pad(w_router.astype(F32), ((0, 0), (0, LANES - N_EXPERTS)))
    wr_hi = wr_pad.astype(BF16)
    br_pad = jnp.pad(b_router.astype(F32), (0, LANES - N_EXPERTS)).reshape(1, LANES)
    x1, h2, route, gates, counts = _out_router(
        o_a.reshape(t, DN_WIDTH), o_b.reshape(t, CA_WIDTH), x2, w_out.astype(BF16),
        norm2_w.reshape(1, d), wr_hi, (wr_pad - wr_hi.astype(F32)).astype(BF16), br_pad, tm=512)

    bm = EXPERT_BLOCK_ROWS
    n_assign = t * TOP_K
    n_blocks = n_assign // bm + N_EXPERTS
    n_slots = n_blocks * bm
    counts_i = counts[0, :N_EXPERTS].astype(jnp.int32)
    padded = (counts_i + bm - 1) // bm * bm
    pad_end = jnp.cumsum(padded)
    pad_start = pad_end - padded
    top_e = route[:, :TOP_K]
    expert_ids = jnp.arange(N_EXPERTS, dtype=jnp.int32)
    start_of = jnp.sum(jnp.where(top_e[..., None] == expert_ids, pad_start, 0), axis=-1)
    pos = start_of + route[:, TOP_K:2 * TOP_K]
    slot_assign = _invert_slots(pos.reshape(-1), n_slots)
    n_out_rows = n_assign + RING * bm
    slot_id = jnp.arange(n_slots, dtype=jnp.int32)
    spare_row = n_assign + (slot_id // bm % RING) * bm + slot_id % bm
    is_real = slot_assign >= 0
    slot_src = jnp.where(is_real, slot_assign // TOP_K, 0)
    slot_dst = jnp.where(is_real, (slot_assign % TOP_K) * t + slot_assign // TOP_K, spare_row)
    block_start = jnp.arange(n_blocks, dtype=jnp.int32) * bm
    block_e = jnp.minimum(jnp.sum((pad_end[None, :] <= block_start[:, None]).astype(jnp.int32), axis=1),
                          N_EXPERTS - 1)

    f = w_gate.shape[-1]
    ys = _experts(h2, slot_src, slot_dst, block_e, n_out_rows,
                  w_gate, b_gate.reshape(N_EXPERTS, 1, f), w_up, b_up.reshape(N_EXPERTS, 1, f),
                  w_down, b_down.reshape(N_EXPERTS, 1, d), bm)
    out = _combine(ys, x1, gates, out_norm_w.reshape(1, d), tm=512)
    return out.reshape(b, s, d)


def kernel(x, norm1_w, w_in, conv_w, a_log, dt_bias, dn_norm_w, rel_bias, attn_norm_w, w_out, norm2_w, w_router, b_router, w_gate, b_gate, w_up, b_up, w_down, b_down, final_norm_w):
    depth = norm1_w.shape[0]
    assert depth == 1, "the final RMSNorm is fused into the last layer's combine step"
    return _layer(x, norm1_w[0], w_in[0], conv_w[0], a_log[0], dt_bias[0], dn_norm_w[0],
                  rel_bias[0], attn_norm_w[0], w_out[0], norm2_w[0], w_router[0], b_router[0],
                  w_gate[0], b_gate[0], w_up[0], b_up[0], w_down[0], b_down[0], final_norm_w)
```
